```python
import jax, jax.numpy as jnp
from jax import lax
import numpy as np

D_MODEL = 2048
BATCH = 1
SEQ = 16384
DEPTH = 1

CHUNK = 64
Q_BLOCK = 128
ROPE_THETA = 500000.0
ROPE_FRACTION = 4
NORM_EPS = 1e-6

ATTN_HEAD_DIM = 128
ATTN_WIDTH = D_MODEL // 2
ATTN_HEADS = ATTN_WIDTH // ATTN_HEAD_DIM
ATTN_KV_HEADS = max(1, ATTN_HEADS // 4)
KV_WIDTH = ATTN_KV_HEADS * ATTN_HEAD_DIM
IDX_HEADS = 8
IDX_HEAD_DIM = 64
TOPK_MAX = 256

RWKV_HEAD_DIM = 64
RWKV_WIDTH = D_MODEL - ATTN_WIDTH
RWKV_HEADS = RWKV_WIDTH // RWKV_HEAD_DIM
DECAY_LORA = 64
ICLR_LORA = 64
GATE_LORA = 64
GN_EPS = 64e-5

MIX_WIDTH = ATTN_WIDTH + RWKV_WIDTH
D_FF = 5632

PROJ_SPLITS = (ATTN_WIDTH, KV_WIDTH, KV_WIDTH,
               IDX_HEADS * IDX_HEAD_DIM, IDX_HEAD_DIM, IDX_HEADS,
               RWKV_WIDTH, RWKV_WIDTH, RWKV_WIDTH,
               DECAY_LORA, ICLR_LORA, GATE_LORA)
PROJ_WIDTH = sum(PROJ_SPLITS)

kernel_name = 'hymba_dsa_rwkv7_macaron_block'


def rms_norm(x, g):
    x32 = x.astype(jnp.float32)
    y = x32 * lax.rsqrt(jnp.mean(x32 * x32, axis=-1, keepdims=True) + NORM_EPS)
    return (y * g.astype(jnp.float32)).astype(x.dtype)


def swiglu(x, w_gate, w_up, w_down):
    return (jax.nn.silu(x @ w_gate) * (x @ w_up)) @ w_down


def rope_partial(x, pos):
    dh = x.shape[-1]
    rot = dh // ROPE_FRACTION
    half = rot // 2
    inv_freq = ROPE_THETA ** (-(jnp.arange(half, dtype=jnp.float32) * 2.0 / rot))
    ang = pos.astype(jnp.float32)[:, None] * inv_freq[None, :]
    cos = jnp.cos(ang)[None, :, None, :]
    sin = jnp.sin(ang)[None, :, None, :]
    x32 = x.astype(jnp.float32)
    x1, x2, rest = x32[..., :half], x32[..., half:rot], x32[..., rot:]
    out = jnp.concatenate([x1 * cos - x2 * sin, x2 * cos + x1 * sin, rest], axis=-1)
    return out.astype(x.dtype)


def split_columns(u):
    idx = [int(v) for v in np.cumsum(PROJ_SPLITS)[:-1]]
    return jnp.split(u, idx, axis=-1)


def dsa_attention(q, k, v, qi, ki, wi):
    B, S = q.shape[0], q.shape[1]
    G, R, Dh = ATTN_KV_HEADS, ATTN_HEADS // ATTN_KV_HEADS, ATTN_HEAD_DIM
    k_sel = min(TOPK_MAX, S // 4)
    nblk = S // Q_BLOCK
    key_chunk = jnp.arange(S) // CHUNK
    idx_scale = (IDX_HEADS ** -0.5) * (IDX_HEAD_DIM ** -0.5)
    att_scale = Dh ** -0.5
    ki32 = ki.astype(jnp.float32)

    def to_blocks(a):
        return jnp.moveaxis(a.reshape((B, nblk, Q_BLOCK) + a.shape[2:]), 1, 0)

    qb = to_blocks(q.reshape(B, S, G, R, Dh))
    qib = to_blocks(qi)
    wib = to_blocks(wi)
    tb = jnp.arange(S).reshape(nblk, Q_BLOCK)

    def block(args):
        q_blk, qi_blk, wi_blk, t_blk = args
        s_idx = jnp.einsum('bqhd,bsd->bqhs', qi_blk.astype(jnp.float32), ki32)
        score = jnp.einsum('bqh,bqhs->bqs', wi_blk.astype(jnp.float32) * idx_scale,
                           jax.nn.relu(s_idx))
        q_chunk = t_blk // CHUNK
        admissible = key_chunk[None, :] <= q_chunk[:, None]
        score = jnp.where(admissible[None], score, -jnp.inf)
        _, sel = lax.top_k(score, k_sel)
        valid = (sel // CHUNK) <= q_chunk[None, :, None]
        k_g = jax.vmap(lambda kb, ib: kb[ib])(k, sel)
        v_g = jax.vmap(lambda vb, ib: vb[ib])(v, sel)
        s = jnp.einsum('bqgrd,bqngd->bqgrn', q_blk.astype(jnp.float32),
                       k_g.astype(jnp.float32)) * att_scale
        s = jnp.where(valid[:, :, None, None, :], s, -jnp.inf)
        p = jax.nn.softmax(s, axis=-1)
        o = jnp.einsum('bqgrn,bqngd->bqgrd', p, v_g.astype(jnp.float32))
        return o.astype(q.dtype)

    out = lax.map(block, (qb, qib, wib, tb))
    return jnp.moveaxis(out, 0, 1).reshape(B, S, ATTN_WIDTH)


def token_shift_lerp(p, mu):
    prev = jnp.pad(p, ((0, 0), (1, 0), (0, 0)))[:, :-1]
    return p + (prev - p) * mu


def rwkv7_time_mix(r_p, k_p, v_p, wd_p, ad_p, gd_p, mu_r, mu_k, mu_v, mu_w, mu_a, mu_g,
                   w0, w2, a0, a2, g2, k_k, k_a, r_k, gn_gain, gn_bias):
    B, S = r_p.shape[0], r_p.shape[1]
    H, N = RWKV_HEADS, RWKV_HEAD_DIM
    dt = r_p.dtype
    r = token_shift_lerp(r_p, mu_r)
    k = token_shift_lerp(k_p, mu_k)
    v = token_shift_lerp(v_p, mu_v)
    xw = token_shift_lerp(wd_p, mu_w)
    xa = token_shift_lerp(ad_p, mu_a)
    xg = token_shift_lerp(gd_p, mu_g)

    w = -jax.nn.softplus(-(w0 + jnp.tanh(xw) @ w2)) - 0.5
    decay = jnp.exp(-jnp.exp(w.astype(jnp.float32)))
    a = jax.nn.sigmoid((a0 + xa @ a2).astype(jnp.float32))
    g = jax.nn.sigmoid(xg) @ g2

    hd = lambda t: t.reshape(B, S, H, N).astype(jnp.float32)
    r, k, v, a, decay = hd(r), hd(k), hd(v), hd(a), hd(decay)
    kk = k * k_k.reshape(H, N).astype(jnp.float32)
    kk = kk / jnp.maximum(jnp.sqrt(jnp.sum(kk * kk, axis=-1, keepdims=True)), 1e-12)
    k = k * (1.0 + (a - 1.0) * k_a.reshape(H, N).astype(jnp.float32))

    def step(state, xs):
        r_t, w_t, k_t, v_t, a_t, b_t = xs
        sa = jnp.einsum('bhvk,bhk->bhv', state, a_t)
        state = (state * w_t[:, :, None, :] + sa[..., None] * b_t[:, :, None, :]
                 + v_t[..., None] * k_t[:, :, None, :])
        y_t = jnp.einsum('bhvk,bhk->bhv', state, r_t)
        return state, y_t

    tf = lambda t: jnp.moveaxis(t, 1, 0)
    state0 = jnp.zeros((B, H, N, N), jnp.float32)
    _, y = lax.scan(step, state0, (tf(r), tf(decay), tf(k), tf(v), tf(-kk), tf(kk * a)))
    y = jnp.moveaxis(y, 0, 1)

    mean = jnp.mean(y, axis=-1, keepdims=True)
    var = jnp.mean(jnp.square(y - mean), axis=-1, keepdims=True)
    y = ((y - mean) * lax.rsqrt(var + GN_EPS)).reshape(B, S, RWKV_WIDTH)
    y = y * gn_gain.astype(jnp.float32) + gn_bias.astype(jnp.float32)
    bonus = jnp.sum(r * k * r_k.astype(jnp.float32), axis=-1, keepdims=True) * v
    y = y + bonus.reshape(B, S, RWKV_WIDTH)
    return (y * g.astype(jnp.float32)).astype(dt)


def setup_inputs(seed: int = 0) -> dict:
    key = jax.random.key(seed)
    ks = iter(jax.random.split(key, 40))
    L, D, F = DEPTH, D_MODEL, D_FF
    nrm = lambda shape, scale: jax.random.normal(next(ks), shape, jnp.float32) * scale
    gain = lambda shape: 1.0 + nrm(shape, 0.02)
    unif = lambda shape, lo, hi: jax.random.uniform(next(ks), shape, jnp.float32, lo, hi)
    return {
        'x': nrm((BATCH, SEQ, D), 1.0),
        'ffn1_norm': gain((L, D)),
        'ffn1_w_gate': nrm((L, D, F), D ** -0.5),
        'ffn1_w_up': nrm((L, D, F), D ** -0.5),
        'ffn1_w_down': nrm((L, F, D), F ** -0.5),
        'mix_norm': gain((L, D)),
        'w_in': nrm((L, D, PROJ_WIDTH), D ** -0.5),
        'w_out': nrm((L, MIX_WIDTH, D), MIX_WIDTH ** -0.5),
        'rwkv_mu_r': unif((L, RWKV_WIDTH), 0.0, 1.0),
        'rwkv_mu_k': unif((L, RWKV_WIDTH), 0.0, 1.0),
        'rwkv_mu_v': unif((L, RWKV_WIDTH), 0.0, 1.0),
        'rwkv_mu_w': unif((L, DECAY_LORA), 0.0, 1.0),
        'rwkv_mu_a': unif((L, ICLR_LORA), 0.0, 1.0),
        'rwkv_mu_g': unif((L, GATE_LORA), 0.0, 1.0),
        'rwkv_w0': unif((L, RWKV_WIDTH), -6.5, -1.5),
        'rwkv_w2': nrm((L, DECAY_LORA, RWKV_WIDTH), 0.1 * DECAY_LORA ** -0.5),
        'rwkv_a0': nrm((L, RWKV_WIDTH), 0.1),
        'rwkv_a2': nrm((L, ICLR_LORA, RWKV_WIDTH), 0.1 * ICLR_LORA ** -0.5),
        'rwkv_g2': nrm((L, GATE_LORA, RWKV_WIDTH), GATE_LORA ** -0.5),
        'rwkv_k_k': 0.85 + nrm((L, RWKV_WIDTH), 0.02),
        'rwkv_k_a': gain((L, RWKV_WIDTH)),
        'rwkv_r_k': nrm((L, RWKV_HEADS, RWKV_HEAD_DIM), 0.1),
        'rwkv_gn_gain': gain((L, RWKV_WIDTH)),
        'rwkv_gn_bias': nrm((L, RWKV_WIDTH), 0.02),
        'ffn2_norm': gain((L, D)),
        'ffn2_w_gate': nrm((L, D, F), D ** -0.5),
        'ffn2_w_up': nrm((L, D, F), D ** -0.5),
        'ffn2_w_down': nrm((L, F, D), F ** -0.5),
        'final_norm': gain((D,)),
    }


def reference(x, ffn1_norm, ffn1_w_gate, ffn1_w_up, ffn1_w_down, mix_norm, w_in, w_out,
              rwkv_mu_r, rwkv_mu_k, rwkv_mu_v, rwkv_mu_w, rwkv_mu_a, rwkv_mu_g,
              rwkv_w0, rwkv_w2, rwkv_a0, rwkv_a2, rwkv_g2, rwkv_k_k, rwkv_k_a, rwkv_r_k,
              rwkv_gn_gain, rwkv_gn_bias, ffn2_norm, ffn2_w_gate, ffn2_w_up, ffn2_w_down,
              final_norm):
    B, S, _ = x.shape
    pos = jnp.arange(S)
    h = x
    for l in range(DEPTH):
        h = h + 0.5 * swiglu(rms_norm(h, ffn1_norm[l]), ffn1_w_gate[l], ffn1_w_up[l], ffn1_w_down[l])

        u = rms_norm(h, mix_norm[l]) @ w_in[l]
        q, k, v, qi, ki, wi, rr, rk, rv, wd, ad, gd = split_columns(u)

        q = rope_partial(q.reshape(B, S, ATTN_HEADS, ATTN_HEAD_DIM), pos)
        k = rope_partial(k.reshape(B, S, ATTN_KV_HEADS, ATTN_HEAD_DIM), pos)
        v = v.reshape(B, S, ATTN_KV_HEADS, ATTN_HEAD_DIM)
        qi = rope_partial(qi.reshape(B, S, IDX_HEADS, IDX_HEAD_DIM), pos)
        ki = rope_partial(ki.reshape(B, S, 1, IDX_HEAD_DIM), pos)[:, :, 0]
        attn_out = dsa_attention(q, k, v, qi, ki, wi)

        rwkv_out = rwkv7_time_mix(rr, rk, rv, wd, ad, gd,
                                  rwkv_mu_r[l], rwkv_mu_k[l], rwkv_mu_v[l],
                                  rwkv_mu_w[l], rwkv_mu_a[l], rwkv_mu_g[l],
                                  rwkv_w0[l], rwkv_w2[l], rwkv_a0[l], rwkv_a2[l], rwkv_g2[l],
                                  rwkv_k_k[l], rwkv_k_a[l], rwkv_r_k[l],
                                  rwkv_gn_gain[l], rwkv_gn_bias[l])

        h = h + jnp.concatenate([attn_out, rwkv_out], axis=-1) @ w_out[l]

        h = h + 0.5 * swiglu(rms_norm(h, ffn2_norm[l]), ffn2_w_gate[l], ffn2_w_up[l], ffn2_w_down[l])
    return rms_norm(h, final_norm)
```

```python
import functools

import jax
import jax.numpy as jnp
import numpy as np
from jax import lax
from jax.experimental import pallas as pl
from jax.experimental.pallas import tpu as pltpu

F32 = jnp.float32
BF16 = jnp.bfloat16
I32 = jnp.int32

D_MODEL = 2048
CHUNK = 64
ROPE_THETA = 500000.0
ROPE_FRACTION = 4
NORM_EPS = 1e-6
ATTN_HEAD_DIM = 128
ATTN_WIDTH = 1024
ATTN_HEADS = 8
ATTN_KV_HEADS = 2
KV_WIDTH = 256
IDX_HEADS = 8
IDX_HEAD_DIM = 64
TOPK_MAX = 256
RWKV_HEAD_DIM = 64
RWKV_WIDTH = 1024
RWKV_HEADS = 16
LORA = 64
GN_EPS = 64e-5
D_FF = 5632

LANES = 128
VMEM_LIMIT = 56 * 1024 * 1024

U_WIDTH = 5632
U_Q = 0
U_KV = 1024
U_QI = 1536
U_RR = 2048
U_RK = 3072
U_RV = 4096
U_MISC = 5120

RW_CHUNK = 64
INT_MIN = -2 ** 31
NEG_BIG = -1e30


def _rms(x, g):
    return x * lax.rsqrt(jnp.mean(x * x, axis=-1, keepdims=True) + NORM_EPS) * g


def _dot(a, b):
    return jnp.dot(a, b, preferred_element_type=F32)


def _dot_nt(a, b):
    return lax.dot_general(a, b, (((1,), (1,)), ((), ())), preferred_element_type=F32)


def _split3(x):
    hi = x.astype(BF16)
    r1 = x - hi.astype(F32)
    mid = r1.astype(BF16)
    lo = (r1 - mid.astype(F32)).astype(BF16)
    return hi, mid, lo


def _dot_exact_rhs(x, m):
    hi, mid, lo = _split3(x)
    return _dot(hi, m) + _dot(mid, m) + _dot(lo, m)


def _dot_exact_lhs(m, x):
    hi, mid, lo = _split3(x)
    return _dot(m, hi) + _dot(m, mid) + _dot(m, lo)


def _split2(x):
    hi = x.astype(BF16)
    lo = (x - hi.astype(F32)).astype(BF16)
    return hi, lo


def _mm3(a, b):
    ah, al = _split2(a)
    bh, bl = _split2(b)
    return _dot(ah, bh) + _dot(ah, bl) + _dot(al, bh)


def _mm3_nt(a, b):
    ah, al = _split2(a)
    bh, bl = _split2(b)
    return _dot_nt(ah, bh) + _dot_nt(ah, bl) + _dot_nt(al, bh)


def _ffn_body(x_ref, g_ref, wg_ref, wu_ref, wd_ref, fg_ref, o_ref, xn_ref, acc_ref, *, final_norm):
    j = pl.program_id(1)

    @pl.when(j == 0)
    def _():
        xn_ref[...] = _rms(x_ref[...], g_ref[...]).astype(BF16)
        acc_ref[...] = jnp.zeros_like(acc_ref)

    xn = xn_ref[...]
    gt = _dot(xn, wg_ref[...])
    ut = _dot(xn, wu_ref[...])
    act = (gt * jax.nn.sigmoid(gt) * ut).astype(BF16)
    acc_ref[...] += _dot(act, wd_ref[...])

    @pl.when(j == pl.num_programs(1) - 1)
    def _():
        h = x_ref[...] + 0.5 * acc_ref[...]
        if final_norm:
            h = _rms(h, fg_ref[...])
        o_ref[...] = h


def _ffn(x, gain, wg, wu, wd, fgain, *, final_norm, tm=512, tf=512):
    s, d = x.shape
    f = wg.shape[1]
    return pl.pallas_call(
        functools.partial(_ffn_body, final_norm=final_norm),
        name="ffn_final" if final_norm else "ffn",
        grid=(s // tm, f // tf),
        in_specs=[
            pl.BlockSpec((tm, d), lambda i, j: (i, 0)),
            pl.BlockSpec((1, d), lambda i, j: (0, 0)),
            pl.BlockSpec((d, tf), lambda i, j: (0, j)),
            pl.BlockSpec((d, tf), lambda i, j: (0, j)),
            pl.BlockSpec((tf, d), lambda i, j: (j, 0)),
            pl.BlockSpec((1, d), lambda i, j: (0, 0)),
        ],
        out_specs=pl.BlockSpec((tm, d), lambda i, j: (i, 0)),
        out_shape=jax.ShapeDtypeStruct((s, d), F32),
        scratch_shapes=[pltpu.VMEM((tm, d), BF16), pltpu.VMEM((tm, d), F32)],
        compiler_params=pltpu.CompilerParams(
            dimension_semantics=("parallel", "arbitrary"), vmem_limit_bytes=VMEM_LIMIT),
    )(x, gain, wg, wu, wd, fgain)


def _inproj_body(x_ref, g_ref, w_ref, o_ref, xn_ref):
    @pl.when(pl.program_id(1) == 0)
    def _():
        xn_ref[...] = _rms(x_ref[...], g_ref[...]).astype(BF16)

    o_ref[...] = _dot(xn_ref[...], w_ref[...])


def _inproj(h, gain, w, *, tm=512, tn=512):
    s, d = h.shape
    n = w.shape[1]
    return pl.pallas_call(
        _inproj_body,
        name="inproj",
        grid=(s // tm, n // tn),
        in_specs=[
            pl.BlockSpec((tm, d), lambda i, j: (i, 0)),
            pl.BlockSpec((1, d), lambda i, j: (0, 0)),
            pl.BlockSpec((d, tn), lambda i, j: (0, j)),
        ],
        out_specs=pl.BlockSpec((tm, tn), lambda i, j: (i, j)),
        out_shape=jax.ShapeDtypeStruct((s, n), F32),
        scratch_shapes=[pltpu.VMEM((tm, d), BF16)],
        compiler_params=pltpu.CompilerParams(
            dimension_semantics=("parallel", "arbitrary"), vmem_limit_bytes=VMEM_LIMIT),
    )(h, gain, w)


def _rope_tables(s, head_dim):
    rot = head_dim // ROPE_FRACTION
    half = rot // 2
    inv_freq = ROPE_THETA ** (-(jnp.arange(half, dtype=F32) * 2.0 / rot))
    ang = jnp.arange(s, dtype=F32)[:, None] * inv_freq[None, :]
    cos, sin = jnp.cos(ang), jnp.sin(ang)
    ones = jnp.ones((s, head_dim - rot), F32)
    zeros = jnp.zeros((s, head_dim - rot), F32)
    zh = jnp.zeros((s, half), F32)
    c = jnp.concatenate([cos, cos, ones], axis=1)
    sp = jnp.concatenate([zh, sin, zeros], axis=1)
    sm = jnp.concatenate([-sin, zh, zeros], axis=1)
    reps = LANES // head_dim
    return tuple(jnp.tile(t, (1, reps)) for t in (c, sp, sm))


def _rope128(x, c, sp, sm, half):
    return x * c + pltpu.roll(x, half, 1) * sp + pltpu.roll(x, LANES - half, 1) * sm


def _attn_prep_body(q_ref, kv_ref, qi_ref, misc_ref, c1_ref, sp1_ref, sm1_ref, c2_ref, sp2_ref, sm2_ref,
                    qo_ref, ko_ref, vo_ref, qio_ref, kio_ref, wio_ref):
    c1, sp1, sm1 = c1_ref[...], sp1_ref[...], sm1_ref[...]
    c2, sp2, sm2 = c2_ref[...], sp2_ref[...], sm2_ref[...]
    att_scale = ATTN_HEAD_DIM ** -0.5
    for b in range(ATTN_WIDTH // LANES):
        sl = slice(b * LANES, (b + 1) * LANES)
        qo_ref[:, sl] = (_rope128(q_ref[:, sl], c1, sp1, sm1, 16) * att_scale).astype(BF16)
    for b in range(KV_WIDTH // LANES):
        sl = slice(b * LANES, (b + 1) * LANES)
        ko_ref[:, sl] = _rope128(kv_ref[:, sl], c1, sp1, sm1, 16).astype(BF16)
    vo_ref[...] = kv_ref[:, KV_WIDTH:].astype(BF16)
    for b in range(IDX_HEADS * IDX_HEAD_DIM // LANES):
        sl = slice(b * LANES, (b + 1) * LANES)
        qio_ref[:, sl] = _rope128(qi_ref[:, sl], c2, sp2, sm2, 8).astype(BF16)
    ki = _rope128(misc_ref[:, :LANES], c2, sp2, sm2, 8)
    kio_ref[...] = ki[:, :IDX_HEAD_DIM].astype(BF16)
    idx_scale = (IDX_HEADS ** -0.5) * (IDX_HEAD_DIM ** -0.5)
    wio_ref[...] = misc_ref[:, 2 * LANES:3 * LANES] * idx_scale


def _attn_prep(u, tabs128, tabs64, *, tm=512):
    s = u.shape[0]
    tab_spec = pl.BlockSpec((tm, LANES), lambda i: (i, 0))
    return pl.pallas_call(
        _attn_prep_body,
        name="attn_prep",
        grid=(s // tm,),
        in_specs=[
            pl.BlockSpec((tm, ATTN_WIDTH), lambda i: (i, U_Q // ATTN_WIDTH)),
            pl.BlockSpec((tm, 512), lambda i: (i, U_KV // 512)),
            pl.BlockSpec((tm, 512), lambda i: (i, U_QI // 512)),
            pl.BlockSpec((tm, 512), lambda i: (i, U_MISC // 512)),
        ] + [tab_spec] * 6,
        out_specs=[
            pl.BlockSpec((tm, ATTN_WIDTH), lambda i: (i, 0)),
            pl.BlockSpec((tm, KV_WIDTH), lambda i: (i, 0)),
            pl.BlockSpec((tm, KV_WIDTH), lambda i: (i, 0)),
            pl.BlockSpec((tm, 512), lambda i: (i, 0)),
            pl.BlockSpec((tm, IDX_HEAD_DIM), lambda i: (i, 0)),
            pl.BlockSpec((tm, LANES), lambda i: (i, 0)),
        ],
        out_shape=[
            jax.ShapeDtypeStruct((s, ATTN_WIDTH), BF16),
            jax.ShapeDtypeStruct((s, KV_WIDTH), BF16),
            jax.ShapeDtypeStruct((s, KV_WIDTH), BF16),
            jax.ShapeDtypeStruct((s, 512), BF16),
            jax.ShapeDtypeStruct((s, IDX_HEAD_DIM), BF16),
            jax.ShapeDtypeStruct((s, LANES), F32),
        ],
        compiler_params=pltpu.CompilerParams(
            dimension_semantics=("parallel",), vmem_limit_bytes=VMEM_LIMIT),
    )(u, u, u, u, *tabs128, *tabs64)


def _dsa_body(q_ref, qi_ref, wi_ref, k_ref, v_ref, ki_ref, o_ref, key_ref, m_ref, l_ref, acc_ref,
              *, tq, tk, k_sel):
    qb = pl.program_id(0)
    q_end = (qb + 1) * tq
    n_kt = (q_end + tk - 1) // tk
    n_full = (qb * tq) // tk

    row = lax.broadcasted_iota(I32, (tq, 1), 0) + qb * tq
    key_lim = (row // CHUNK + 1) * CHUNK
    wi = wi_ref[...]

    def score_tile(jt, masked):
        kit = ki_ref[pl.ds(pl.multiple_of(jt * tk, tk), tk), :]
        acc = jnp.zeros((tq, tk), F32)
        for h in range(IDX_HEADS):
            s = _dot_nt(qi_ref[:, h * IDX_HEAD_DIM:(h + 1) * IDX_HEAD_DIM], kit)
            acc = acc + jnp.maximum(s, 0.0) * wi[:, h:h + 1]
        bits = pltpu.bitcast(acc, I32)
        key = bits ^ ((bits >> 31) & 0x7FFFFFFF)
        if masked:
            col = lax.broadcasted_iota(I32, (tq, tk), 1) + jt * tk
            key = jnp.where(col < key_lim, key, INT_MIN)
        key_ref[jt] = key

    def full_tile(jt, c):
        score_tile(jt, False)
        return c

    def diag_tile(jt, c):
        score_tile(jt, True)
        return c

    lax.fori_loop(0, n_full, full_tile, 0)
    lax.fori_loop(n_full, n_kt, diag_tile, 0)

    def bit_step(i, base):
        cand = base + lax.shift_left(jnp.int32(1), 31 - i)

        def count_tile(jt, cnt):
            kt = key_ref[jt]
            for c in range(tk // LANES):
                cnt = cnt + (kt[:, c * LANES:(c + 1) * LANES] >= cand).astype(F32)
            return cnt

        cnt = lax.fori_loop(0, n_kt, count_tile, jnp.zeros((tq, LANES), F32))
        tot = jnp.sum(cnt, axis=1, keepdims=True)
        return jnp.where(tot >= k_sel, cand, base)

    base = lax.fori_loop(0, 32, bit_step, jnp.full((tq, 1), INT_MIN, I32))
    thr = jnp.maximum(base, INT_MIN + 1)

    m_ref[...] = jnp.full(m_ref.shape, NEG_BIG, F32)
    l_ref[...] = jnp.zeros(l_ref.shape, F32)
    acc_ref[...] = jnp.zeros(acc_ref.shape, F32)
    reps = tk // LANES

    def attn_tile(jt, c):
        sel = key_ref[jt] >= thr
        off = pl.multiple_of(jt * tk, tk)
        for g in range(ATTN_KV_HEADS):
            kg = k_ref[pl.ds(off, tk), g * ATTN_HEAD_DIM:(g + 1) * ATTN_HEAD_DIM]
            vg = v_ref[pl.ds(off, tk), g * ATTN_HEAD_DIM:(g + 1) * ATTN_HEAD_DIM]
            for r in range(ATTN_HEADS // ATTN_KV_HEADS):
                h = g * (ATTN_HEADS // ATTN_KV_HEADS) + r
                s = _dot_nt(q_ref[:, h * ATTN_HEAD_DIM:(h + 1) * ATTN_HEAD_DIM], kg)
                s = jnp.where(sel, s, NEG_BIG)
                m_prev = m_ref[h]
                m_cur = jnp.maximum(m_prev, jnp.max(s, axis=1, keepdims=True))
                alpha = jnp.exp(m_prev - m_cur)
                p = jnp.exp(s - jnp.concatenate([m_cur] * reps, axis=1))
                l_ref[h] = alpha * l_ref[h] + jnp.sum(p, axis=1, keepdims=True)
                acc_ref[h] = acc_ref[h] * alpha + _dot(p.astype(BF16), vg)
                m_ref[h] = m_cur
        return c

    lax.fori_loop(0, n_kt, attn_tile, 0)
    for h in range(ATTN_HEADS):
        o_ref[:, h * ATTN_HEAD_DIM:(h + 1) * ATTN_HEAD_DIM] = (acc_ref[h] / l_ref[h]).astype(BF16)


def _dsa(q, qi, wi, k, v, ki, *, tq=128, tk=512):
    s = q.shape[0]
    tk = min(tk, s)
    k_sel = min(TOPK_MAX, s // 4)
    whole = lambda shape: pl.BlockSpec(shape, lambda i: (0, 0))
    return pl.pallas_call(
        functools.partial(_dsa_body, tq=tq, tk=tk, k_sel=k_sel),
        name="dsa",
        grid=(s // tq,),
        in_specs=[
            pl.BlockSpec((tq, ATTN_WIDTH), lambda i: (i, 0)),
            pl.BlockSpec((tq, 512), lambda i: (i, 0)),
            pl.BlockSpec((tq, LANES), lambda i: (i, 0)),
            whole((s, KV_WIDTH)),
            whole((s, KV_WIDTH)),
            whole((s, IDX_HEAD_DIM)),
        ],
        out_specs=pl.BlockSpec((tq, ATTN_WIDTH), lambda i: (i, 0)),
        out_shape=jax.ShapeDtypeStruct((s, ATTN_WIDTH), BF16),
        scratch_shapes=[
            pltpu.VMEM((s // tk, tq, tk), I32),
            pltpu.VMEM((ATTN_HEADS, tq, LANES), F32),
            pltpu.VMEM((ATTN_HEADS, tq, LANES), F32),
            pltpu.VMEM((ATTN_HEADS, tq, ATTN_HEAD_DIM), F32),
        ],
        compiler_params=pltpu.CompilerParams(
            dimension_semantics=("parallel",), vmem_limit_bytes=VMEM_LIMIT),
    )(q, qi, wi, k, v, ki)


def _token_shift(x, prev8, first):
    rolled = pltpu.roll(x, 1, 0)
    prev_row = jnp.where(first, 0.0, prev8[7:8, :])
    row = lax.broadcasted_iota(I32, x.shape, 0)
    return jnp.where(row == 0, jnp.broadcast_to(prev_row, x.shape), rolled)


def _rwkv_prep_body(rr_ref, rk_ref, rv_ref, ms_ref, prr_ref, prk_ref, prv_ref, pms_ref,
                    mur_ref, muk_ref, muv_ref, mum_ref, w0_ref, w2_ref, a0_ref, a2_ref, g2_ref,
                    kk_ref, ka_ref, rkk_ref, bd_ref, tri_ref, blk_ref,
                    at_ref, rt_ref, bt_ref, kt_ref, v_ref, bdt_ref, kdt_ref, pct_ref, bonus_ref, g_ref):
    first = pl.program_id(0) == 0

    def lerp(x_ref, p_ref, mu_ref):
        x = x_ref[...]
        return x + (_token_shift(x, p_ref[...], first) - x) * mu_ref[...]

    r = lerp(rr_ref, prr_ref, mur_ref)
    k = lerp(rk_ref, prk_ref, muk_ref)
    v = lerp(rv_ref, prv_ref, muv_ref)
    ms = lerp(ms_ref, pms_ref, mum_ref)
    xw = ms[:, :LANES]
    xag = ms[:, LANES:]

    w = w0_ref[...] + _dot(jnp.tanh(xw).astype(BF16), w2_ref[...])
    z = -w
    softplus = jnp.maximum(z, 0.0) + jnp.log1p(jnp.exp(-jnp.abs(z)))
    lw = -jnp.exp(-softplus - 0.5)
    a_lr = jax.nn.sigmoid(a0_ref[...] + _dot(xag.astype(BF16), a2_ref[...]))
    g_ref[...] = _dot(jax.nn.sigmoid(xag).astype(BF16), g2_ref[...])

    bd = bd_ref[...]
    kk = k * kk_ref[...]
    nrm = jnp.sqrt(_dot_exact_rhs(kk * kk, bd))
    kk = kk / jnp.maximum(nrm, 1e-12)
    k = k * (1.0 + (a_lr - 1.0) * ka_ref[...])
    bonus_ref[...] = _dot_exact_rhs(r * k * rkk_ref[...], bd) * v

    cum = _dot_exact_lhs(tri_ref[...], lw)
    tot = _dot_exact_lhs(blk_ref[...], lw)
    e_neg = jnp.exp(-cum)
    e_dec = jnp.exp(tot - cum)
    b = kk * a_lr
    at_ref[...] = -kk * jnp.exp(cum - lw)
    rt_ref[...] = r * jnp.exp(cum)
    bt_ref[...] = b * e_neg
    kt_ref[...] = k * e_neg
    v_ref[...] = v
    bdt_ref[...] = (b * e_dec).T
    kdt_ref[...] = (k * e_dec).T
    pct_ref[...] = jnp.exp(tot).T


def _rwkv_prep(u, p, *, tm=512):
    s = u.shape[0]
    w = RWKV_WIDTH
    t8 = tm // 8
    tile = lambda col: pl.BlockSpec((tm, w), lambda i: (i, col // w))
    prev = lambda col: pl.BlockSpec((8, w), lambda i: (jnp.maximum(i * t8 - 1, 0), col // w))
    vec = lambda n: pl.BlockSpec((1, n), lambda i: (0, 0))
    mat = lambda a, b: pl.BlockSpec((a, b), lambda i: (0, 0))
    tm_spec = pl.BlockSpec((tm, w), lambda i: (i, 0))
    cm_spec = pl.BlockSpec((w, tm), lambda i: (0, i))
    outs = pl.pallas_call(
        _rwkv_prep_body,
        name="rwkv_prep",
        grid=(s // tm,),
        in_specs=[
            tile(U_RR), tile(U_RK), tile(U_RV),
            pl.BlockSpec((tm, 256), lambda i: (i, U_MISC // 256)),
            prev(U_RR), prev(U_RK), prev(U_RV),
            pl.BlockSpec((8, 256), lambda i: (jnp.maximum(i * t8 - 1, 0), U_MISC // 256)),
            vec(w), vec(w), vec(w), vec(256),
            vec(w), mat(LANES, w), vec(w), mat(LANES, w), mat(LANES, w),
            vec(w), vec(w), vec(w),
            mat(w, w), mat(tm, tm), mat(tm, tm),
        ],
        out_specs=[tm_spec] * 5 + [cm_spec] * 3 + [tm_spec] * 2,
        out_shape=[jax.ShapeDtypeStruct((s, w), F32)] * 5
        + [jax.ShapeDtypeStruct((w, s), F32)] * 3
        + [jax.ShapeDtypeStruct((s, w), F32)] * 2,
        compiler_params=pltpu.CompilerParams(
            dimension_semantics=("parallel",), vmem_limit_bytes=VMEM_LIMIT),
    )(u, u, u, u, u, u, u, u,
      p["mu_r"], p["mu_k"], p["mu_v"], p["mu_m"], p["w0"], p["w2"], p["a0"], p["a2"], p["g2"],
      p["k_k"], p["k_a"], p["r_k"], p["bd"], p["tri"], p["blk"])
    return outs


def _rwkv_scan_body(at_ref, rt_ref, bt_ref, kt_ref, v_ref, bdt_ref, kdt_ref, pct_ref, y_ref, st_ref):
    c = pl.program_id(0)
    hp = pl.program_id(1)
    n = RWKV_HEAD_DIM
    cl = RW_CHUNK

    @pl.when(c == 0)
    def _():
        st_ref[hp] = jnp.zeros((2, n, n), F32)

    ri = lax.broadcasted_iota(I32, (cl, cl), 0)
    ci = lax.broadcasted_iota(I32, (cl, cl), 1)
    strict = ri > ci
    incl = ri >= ci
    same16 = (ri // 16) == (ci // 16)
    eye = (ri == ci).astype(F32)

    for hh in range(2):
        ls = slice(hh * n, (hh + 1) * n)
        st = st_ref[hp, hh]
        for cc in range(LANES // cl):
            ts = slice(cc * cl, (cc + 1) * cl)
            at, rt = at_ref[ts, ls], rt_ref[ts, ls]
            bt, kt = bt_ref[ts, ls], kt_ref[ts, ls]
            v = v_ref[ts, ls]
            ar = jnp.concatenate([at, rt], axis=0)
            g = _mm3_nt(ar, jnp.concatenate([bt, kt], axis=0))
            l = jnp.where(strict, g[:cl, :cl], 0.0)
            a_ak = jnp.where(strict, g[:cl, cl:], 0.0)
            a_rb = jnp.where(incl, g[cl:, :cl], 0.0)
            a_rk = jnp.where(incl, g[cl:, cl:], 0.0)

            ld = jnp.where(same16, l, 0.0)
            lo = l - ld
            d = eye + ld
            pw = ld
            for _ in range(3):
                pw = _mm3(pw, pw)
                d = d + _mm3(d, pw)
            m = _mm3(d, lo)
            f = d + _mm3(_mm3(m, m), d)
            t = f + _mm3(m, f)

            x = _mm3(ar, st)
            wv = _mm3(jnp.concatenate([a_ak, a_rk], axis=0), v)
            u = _mm3(t, x[:cl] + wv[:cl])
            y_ref[ts, ls] = x[cl:] + wv[cl:] + _mm3(a_rb, u)
            dk = jnp.concatenate([bdt_ref[ls, ts], kdt_ref[ls, ts]], axis=1)
            st = pct_ref[ls, ts] * st + _mm3(dk, jnp.concatenate([u, v], axis=0))
        st_ref[hp, hh] = st


def _rwkv_scan(at, rt, bt, kt, v, bdt, kdt, pct):
    s, w = at.shape
    tmaj = pl.BlockSpec((LANES, LANES), lambda c, h: (c, h))
    cmaj = pl.BlockSpec((LANES, LANES), lambda c, h: (h, c))
    return pl.pallas_call(
        _rwkv_scan_body,
        name="rwkv_scan",
        grid=(s // LANES, w // LANES),
        in_specs=[tmaj] * 5 + [cmaj] * 3,
        out_specs=tmaj,
        out_shape=jax.ShapeDtypeStruct((s, w), F32),
        scratch_shapes=[pltpu.VMEM((w // LANES, 2, RWKV_HEAD_DIM, RWKV_HEAD_DIM), F32)],
        compiler_params=pltpu.CompilerParams(
            dimension_semantics=("arbitrary", "arbitrary"), vmem_limit_bytes=VMEM_LIMIT),
    )(at, rt, bt, kt, v, bdt, kdt, pct)


def _outproj_body(h_ref, attn_ref, y_ref, bonus_ref, g_ref, gain_ref, bias_ref, bd_ref, wa_ref, wr_ref, o_ref):
    y = y_ref[...]
    bd = bd_ref[...]
    inv_n = 1.0 / RWKV_HEAD_DIM
    mean = _dot_exact_rhs(y, bd) * inv_n
    yc = y - mean
    var = _dot_exact_rhs(yc * yc, bd) * inv_n
    yn = yc * lax.rsqrt(var + GN_EPS) * gain_ref[...] + bias_ref[...]
    rw = ((yn + bonus_ref[...]) * g_ref[...]).astype(BF16)
    o_ref[...] = h_ref[...] + _dot(attn_ref[...], wa_ref[...]) + _dot(rw, wr_ref[...])


def _outproj(h, attn, y, bonus, g, gain, bias, bd, wa, wr, *, tm=256):
    s, d = h.shape
    w = RWKV_WIDTH
    row = lambda n: pl.BlockSpec((tm, n), lambda i: (i, 0))
    whole = lambda a, b: pl.BlockSpec((a, b), lambda i: (0, 0))
    return pl.pallas_call(
        _outproj_body,
        name="outproj",
        grid=(s // tm,),
        in_specs=[row(d), row(w), row(w), row(w), row(w), whole(1, w), whole(1, w), whole(w, w),
                  whole(ATTN_WIDTH, d), whole(w, d)],
        out_specs=row(d),
        out_shape=jax.ShapeDtypeStruct((s, d), F32),
        compiler_params=pltpu.CompilerParams(
            dimension_semantics=("parallel",), vmem_limit_bytes=VMEM_LIMIT),
    )(h, attn, y, bonus, g, gain, bias, bd, wa, wr)


def _relayout_w_in(w_in):
    d = w_in.shape[0]
    o = np.cumsum([0, 1024, 256, 256, 512, 64, 8, 1024, 1024, 1024, 64, 64, 64])
    seg = lambda i: w_in[:, o[i]:o[i + 1]]
    q, k, v, qi, ki, wi, rr, rk, rv, wd, ad, gd = (seg(i) for i in range(12))
    z = lambda n: jnp.zeros((d, n), w_in.dtype)
    cols = [q, k, v, qi, rr, rk, rv, ki, wd, ad, gd, wi, z(LANES - 8), z(LANES)]
    return jnp.concatenate(cols, axis=1).astype(BF16)


def _pad_rows(w2, lo):
    z = jnp.zeros_like(w2)
    return (jnp.concatenate([w2, z], axis=0) if lo == 0 else jnp.concatenate([z, w2], axis=0)).astype(BF16)


def kernel(x, ffn1_norm, ffn1_w_gate, ffn1_w_up, ffn1_w_down, mix_norm, w_in, w_out, rwkv_mu_r, rwkv_mu_k, rwkv_mu_v, rwkv_mu_w, rwkv_mu_a, rwkv_mu_g, rwkv_w0, rwkv_w2, rwkv_a0, rwkv_a2, rwkv_g2, rwkv_k_k, rwkv_k_a, rwkv_r_k, rwkv_gn_gain, rwkv_gn_bias, ffn2_norm, ffn2_w_gate, ffn2_w_up, ffn2_w_down, final_norm):
    b, s, d = x.shape
    assert b == 1 and d == D_MODEL and ffn1_norm.shape[0] == 1
    h = x[0]
    row = lambda a: a.reshape(1, -1).astype(F32)
    z64 = jnp.zeros((1, LORA), F32)

    tm_prep = 256
    tix = np.arange(tm_prep)
    same_chunk = (tix[:, None] // RW_CHUNK) == (tix[None, :] // RW_CHUNK)
    hix = np.arange(RWKV_WIDTH) // RWKV_HEAD_DIM
    prep = dict(
        mu_r=row(rwkv_mu_r[0]), mu_k=row(rwkv_mu_k[0]), mu_v=row(rwkv_mu_v[0]),
        mu_m=jnp.concatenate([z64, row(rwkv_mu_w[0]), row(rwkv_mu_a[0]), row(rwkv_mu_g[0])], axis=1),
        w0=row(rwkv_w0[0]), w2=_pad_rows(rwkv_w2[0], LORA),
        a0=row(rwkv_a0[0]), a2=_pad_rows(rwkv_a2[0], 0), g2=_pad_rows(rwkv_g2[0], LORA),
        k_k=row(rwkv_k_k[0]), k_a=row(rwkv_k_a[0]), r_k=row(rwkv_r_k[0]),
        bd=jnp.asarray(hix[:, None] == hix[None, :], BF16),
        tri=jnp.asarray(same_chunk & (tix[:, None] >= tix[None, :]), BF16),
        blk=jnp.asarray(same_chunk, BF16),
    )

    h1 = _ffn(h, row(ffn1_norm[0]), ffn1_w_gate[0].astype(BF16), ffn1_w_up[0].astype(BF16),
              ffn1_w_down[0].astype(BF16), row(final_norm), final_norm=False)
    u = _inproj(h1, row(mix_norm[0]), _relayout_w_in(w_in[0]))

    q, k, v, qi, ki, wi = _attn_prep(u, _rope_tables(s, ATTN_HEAD_DIM), _rope_tables(s, IDX_HEAD_DIM))
    attn = _dsa(q, qi, wi, k, v, ki)

    at, rt, bt, kt, rv, bdt, kdt, pct, bonus, g = _rwkv_prep(u, prep, tm=tm_prep)
    y = _rwkv_scan(at, rt, bt, kt, rv, bdt, kdt, pct)

    wo = w_out[0].astype(BF16)
    h2 = _outproj(h1, attn, y, bonus, g, row(rwkv_gn_gain[0]), row(rwkv_gn_bias[0]), prep["bd"],
                  wo[:ATTN_WIDTH], wo[ATTN_WIDTH:])
    out = _ffn(h2, row(ffn2_norm[0]), ffn2_w_gate[0].astype(BF16), ffn2_w_up[0].astype(BF16),
               ffn2_w_down[0].astype(BF16), row(final_norm), final_norm=True)
    return out[None]
```

```python
import functools

import jax
import jax.numpy as jnp
import numpy as np
from jax import lax
from jax.experimental import pallas as pl
from jax.experimental.pallas import tpu as pltpu

F32 = jnp.float32
BF16 = jnp.bfloat16
I32 = jnp.int32

D_MODEL = 2048
CHUNK = 64
ROPE_THETA = 500000.0
ROPE_FRACTION = 4
NORM_EPS = 1e-6
ATTN_HEAD_DIM = 128
ATTN_WIDTH = 1024
ATTN_HEADS = 8
ATTN_KV_HEADS = 2
KV_WIDTH = 256
IDX_HEADS = 8
IDX_HEAD_DIM = 64
TOPK_MAX = 256
RWKV_HEAD_DIM = 64
RWKV_WIDTH = 1024
RWKV_HEADS = 16
LORA = 64
GN_EPS = 64e-5
D_FF = 5632

LANES = 128
VMEM_LIMIT = 56 * 1024 * 1024

U_WIDTH = 5632
U_Q = 0
U_KV = 1024
U_QI = 1536
U_RR = 2048
U_RK = 3072
U_RV = 4096
U_MISC = 5120

RW_CHUNK = 64
DSA_KEY_TILE = 512
VT_ROWS = ATTN_HEAD_DIM + 16
CNT_ROWS = 128
I16 = jnp.int16
I16_MIN = -2 ** 15
LOG2E = 1.4426950408889634
INT_MIN = -2 ** 31
NEG_BIG = -1e30


def _rms(x, g):
    return x * lax.rsqrt(jnp.mean(x * x, axis=-1, keepdims=True) + NORM_EPS) * g


def _dot(a, b):
    return jnp.dot(a, b, preferred_element_type=F32)


def _dot_nt(a, b):
    return lax.dot_general(a, b, (((1,), (1,)), ((), ())), preferred_element_type=F32)


def _split3(x):
    hi = x.astype(BF16)
    r1 = x - hi.astype(F32)
    mid = r1.astype(BF16)
    lo = (r1 - mid.astype(F32)).astype(BF16)
    return hi, mid, lo


def _dot_exact_rhs(x, m):
    hi, mid, lo = _split3(x)
    return _dot(hi, m) + _dot(mid, m) + _dot(lo, m)


def _dot_exact_lhs(m, x):
    hi, mid, lo = _split3(x)
    return _dot(m, hi) + _dot(m, mid) + _dot(m, lo)


def _split2(x):
    hi = x.astype(BF16)
    lo = (x - hi.astype(F32)).astype(BF16)
    return hi, lo


def _mm3(a, b):
    ah, al = _split2(a)
    bh, bl = _split2(b)
    return _dot(ah, bh) + _dot(ah, bl) + _dot(al, bh)


def _mm3_nt(a, b):
    ah, al = _split2(a)
    bh, bl = _split2(b)
    return _dot_nt(ah, bh) + _dot_nt(ah, bl) + _dot_nt(al, bh)


def _ffn_body(x_ref, g_ref, wg_ref, wu_ref, wd_ref, fg_ref, o_ref, xn_ref, acc_ref, *, final_norm):
    j = pl.program_id(1)

    @pl.when(j == 0)
    def _():
        xn_ref[...] = _rms(x_ref[...], g_ref[...]).astype(BF16)
        acc_ref[...] = jnp.zeros_like(acc_ref)

    xn = xn_ref[...]
    gt = _dot(xn, wg_ref[...])
    ut = _dot(xn, wu_ref[...])
    act = (gt * jax.nn.sigmoid(gt) * ut).astype(BF16)
    acc_ref[...] += _dot(act, wd_ref[...])

    @pl.when(j == pl.num_programs(1) - 1)
    def _():
        h = x_ref[...] + 0.5 * acc_ref[...]
        if final_norm:
            h = _rms(h, fg_ref[...])
        o_ref[...] = h


def _ffn(x, gain, wg, wu, wd, fgain, *, final_norm, tm=512, tf=512):
    s, d = x.shape
    f = wg.shape[1]
    return pl.pallas_call(
        functools.partial(_ffn_body, final_norm=final_norm),
        name="ffn_final" if final_norm else "ffn",
        grid=(s // tm, f // tf),
        in_specs=[
            pl.BlockSpec((tm, d), lambda i, j: (i, 0)),
            pl.BlockSpec((1, d), lambda i, j: (0, 0)),
            pl.BlockSpec((d, tf), lambda i, j: (0, j)),
            pl.BlockSpec((d, tf), lambda i, j: (0, j)),
            pl.BlockSpec((tf, d), lambda i, j: (j, 0)),
            pl.BlockSpec((1, d), lambda i, j: (0, 0)),
        ],
        out_specs=pl.BlockSpec((tm, d), lambda i, j: (i, 0)),
        out_shape=jax.ShapeDtypeStruct((s, d), F32),
        scratch_shapes=[pltpu.VMEM((tm, d), BF16), pltpu.VMEM((tm, d), F32)],
        compiler_params=pltpu.CompilerParams(
            dimension_semantics=("parallel", "arbitrary"), vmem_limit_bytes=VMEM_LIMIT),
    )(x, gain, wg, wu, wd, fgain)


def _inproj_body(x_ref, g_ref, w_ref, o_ref, xn_ref):
    @pl.when(pl.program_id(1) == 0)
    def _():
        xn_ref[...] = _rms(x_ref[...], g_ref[...]).astype(BF16)

    o_ref[...] = _dot(xn_ref[...], w_ref[...])


def _inproj(h, gain, w, *, tm=512, tn=512):
    s, d = h.shape
    n = w.shape[1]
    return pl.pallas_call(
        _inproj_body,
        name="inproj",
        grid=(s // tm, n // tn),
        in_specs=[
            pl.BlockSpec((tm, d), lambda i, j: (i, 0)),
            pl.BlockSpec((1, d), lambda i, j: (0, 0)),
            pl.BlockSpec((d, tn), lambda i, j: (0, j)),
        ],
        out_specs=pl.BlockSpec((tm, tn), lambda i, j: (i, j)),
        out_shape=jax.ShapeDtypeStruct((s, n), F32),
        scratch_shapes=[pltpu.VMEM((tm, d), BF16)],
        compiler_params=pltpu.CompilerParams(
            dimension_semantics=("parallel", "arbitrary"), vmem_limit_bytes=VMEM_LIMIT),
    )(h, gain, w)


def _rope_tables(s, head_dim):
    rot = head_dim // ROPE_FRACTION
    half = rot // 2
    inv_freq = ROPE_THETA ** (-(jnp.arange(half, dtype=F32) * 2.0 / rot))
    ang = jnp.arange(s, dtype=F32)[:, None] * inv_freq[None, :]
    cos, sin = jnp.cos(ang), jnp.sin(ang)
    ones = jnp.ones((s, head_dim - rot), F32)
    zeros = jnp.zeros((s, head_dim - rot), F32)
    zh = jnp.zeros((s, half), F32)
    c = jnp.concatenate([cos, cos, ones], axis=1)
    sp = jnp.concatenate([zh, sin, zeros], axis=1)
    sm = jnp.concatenate([-sin, zh, zeros], axis=1)
    reps = LANES // head_dim
    return tuple(jnp.tile(t, (1, reps)) for t in (c, sp, sm))


def _rope128(x, c, sp, sm, half):
    return x * c + pltpu.roll(x, half, 1) * sp + pltpu.roll(x, LANES - half, 1) * sm


def _attn_prep_body(q_ref, kv_ref, qi_ref, misc_ref, c1_ref, sp1_ref, sm1_ref, c2_ref, sp2_ref, sm2_ref,
                    qo_ref, ko_ref, vo_ref, qio_ref, kio_ref, wio_ref):
    c1, sp1, sm1 = c1_ref[...], sp1_ref[...], sm1_ref[...]
    c2, sp2, sm2 = c2_ref[...], sp2_ref[...], sm2_ref[...]
    att_scale = ATTN_HEAD_DIM ** -0.5 * LOG2E
    for b in range(ATTN_WIDTH // LANES):
        sl = slice(b * LANES, (b + 1) * LANES)
        qo_ref[:, sl] = (_rope128(q_ref[:, sl], c1, sp1, sm1, 16) * att_scale).astype(BF16)
    for b in range(KV_WIDTH // LANES):
        sl = slice(b * LANES, (b + 1) * LANES)
        ko_ref[:, sl] = _rope128(kv_ref[:, sl], c1, sp1, sm1, 16).astype(BF16)
    vt = kv_ref[:, KV_WIDTH:].T.astype(BF16)
    ones = jnp.ones((VT_ROWS - ATTN_HEAD_DIM, vt.shape[1]), BF16)
    for g in range(ATTN_KV_HEADS):
        vo_ref[0, g * VT_ROWS:g * VT_ROWS + ATTN_HEAD_DIM, :] = vt[g * ATTN_HEAD_DIM:(g + 1) * ATTN_HEAD_DIM]
        vo_ref[0, g * VT_ROWS + ATTN_HEAD_DIM:(g + 1) * VT_ROWS, :] = ones
    for b in range(IDX_HEADS * IDX_HEAD_DIM // LANES):
        sl = slice(b * LANES, (b + 1) * LANES)
        qio_ref[:, sl] = _rope128(qi_ref[:, sl], c2, sp2, sm2, 8).astype(BF16)
    ki = _rope128(misc_ref[:, :LANES], c2, sp2, sm2, 8)
    kio_ref[...] = ki[:, :IDX_HEAD_DIM].astype(BF16)
    idx_scale = (IDX_HEADS ** -0.5) * (IDX_HEAD_DIM ** -0.5)
    wio_ref[...] = (misc_ref[:, 2 * LANES:3 * LANES] * idx_scale).T[:IDX_HEADS]


def _attn_prep(u, tabs128, tabs64, *, tm):
    s = u.shape[0]
    tab_spec = pl.BlockSpec((tm, LANES), lambda i: (i, 0))
    return pl.pallas_call(
        _attn_prep_body,
        name="attn_prep",
        grid=(s // tm,),
        in_specs=[
            pl.BlockSpec((tm, ATTN_WIDTH), lambda i: (i, U_Q // ATTN_WIDTH)),
            pl.BlockSpec((tm, 512), lambda i: (i, U_KV // 512)),
            pl.BlockSpec((tm, 512), lambda i: (i, U_QI // 512)),
            pl.BlockSpec((tm, 512), lambda i: (i, U_MISC // 512)),
        ] + [tab_spec] * 6,
        out_specs=[
            pl.BlockSpec((tm, ATTN_WIDTH), lambda i: (i, 0)),
            pl.BlockSpec((tm, KV_WIDTH), lambda i: (i, 0)),
            pl.BlockSpec((1, ATTN_KV_HEADS * VT_ROWS, tm), lambda i: (i, 0, 0)),
            pl.BlockSpec((tm, 512), lambda i: (i, 0)),
            pl.BlockSpec((tm, IDX_HEAD_DIM), lambda i: (i, 0)),
            pl.BlockSpec((IDX_HEADS, tm), lambda i: (0, i)),
        ],
        out_shape=[
            jax.ShapeDtypeStruct((s, ATTN_WIDTH), BF16),
            jax.ShapeDtypeStruct((s, KV_WIDTH), BF16),
            jax.ShapeDtypeStruct((s // tm, ATTN_KV_HEADS * VT_ROWS, tm), BF16),
            jax.ShapeDtypeStruct((s, 512), BF16),
            jax.ShapeDtypeStruct((s, IDX_HEAD_DIM), BF16),
            jax.ShapeDtypeStruct((IDX_HEADS, s), F32),
        ],
        compiler_params=pltpu.CompilerParams(
            dimension_semantics=("parallel",), vmem_limit_bytes=VMEM_LIMIT),
    )(u, u, u, u, *tabs128, *tabs64)


def _col_reduce(x, op):
    n, w = x.shape
    y = x.reshape(n // 64, 64, w)
    acc = y[0]
    for a in range(1, n // 64):
        acc = op(acc, y[a])
    z = acc.reshape(8, 8, w)
    t = [op(z[2 * i], z[2 * i + 1]) for i in range(4)]
    r = op(op(t[0], t[1]), op(t[2], t[3]))
    red = jnp.max if op is jnp.maximum else jnp.sum
    return red(r, axis=0, keepdims=True)


def _dsa_body(q_ref, qi_ref, wt_ref, k_ref, vt_ref, ki_ref, o_ref,
              key_ref, hi_ref, lo_ref, qs_ref, qis_ref, s_ref, s2_ref, m_ref, acc_ref, *, tq, tk, k_sel):
    qb = pl.program_id(0)
    q_end = (qb + 1) * tq
    n_kt = (q_end + tk - 1) // tk
    n_full = (qb * tq) // tk
    n_pair = ATTN_HEADS // 2

    for h in range(ATTN_HEADS):
        qs_ref[h * tq:(h + 1) * tq, :] = q_ref[:, h * ATTN_HEAD_DIM:(h + 1) * ATTN_HEAD_DIM]
    for h in range(IDX_HEADS):
        qis_ref[h * tq:(h + 1) * tq, :] = qi_ref[:, h * IDX_HEAD_DIM:(h + 1) * IDX_HEAD_DIM]

    qpos = lax.broadcasted_iota(I32, (1, tq), 1) + qb * tq
    key_lim = (qpos // CHUNK + 1) * CHUNK
    wt = wt_ref[...]

    def score_tile(jt, masked):
        kit = ki_ref[pl.ds(pl.multiple_of(jt * tk, tk), tk), :]
        acc = jnp.zeros((tk, tq), F32)
        for p in range(IDX_HEADS // 2):
            s2 = _dot_nt(kit, qis_ref[2 * p * tq:(2 * p + 2) * tq, :])
            acc = acc + jnp.maximum(s2[:, :tq], 0.0) * wt[2 * p:2 * p + 1, :]
            acc = acc + jnp.maximum(s2[:, tq:], 0.0) * wt[2 * p + 1:2 * p + 2, :]
        bits = pltpu.bitcast(acc, I32)
        key = bits ^ ((bits >> 31) & 0x7FFFFFFF)
        if masked:
            kpos = lax.broadcasted_iota(I32, (tk, tq), 0) + jt * tk
            key = jnp.where(kpos < key_lim, key, INT_MIN)
        key_ref[jt] = key
        hi_ref[jt] = (key >> 16).astype(I16)
        lo_ref[jt] = (key ^ 0x8000).astype(I16)

    def full_tile(jt, c):
        score_tile(jt, False)
        return c

    def diag_tile(jt, c):
        score_tile(jt, True)
        return c

    lax.fori_loop(0, n_full, full_tile, 0)
    lax.fori_loop(n_full, n_kt, diag_tile, 0)

    one16 = jnp.ones((tk, tq), I16)
    zero16 = jnp.zeros((tk, tq), I16)

    def fold(cnt, hit):
        hit = hit.reshape(tk // CNT_ROWS, CNT_ROWS, tq)
        for a in range(tk // CNT_ROWS):
            cnt = cnt + hit[a]
        return cnt

    def total(cnt):
        return jnp.sum(cnt.astype(I32), axis=0, keepdims=True)

    def search16(half_ref, need):
        def bit_step(i, base):
            cand = base + lax.shift_left(jnp.int32(1), 15 - i)
            cand16 = cand.astype(I16)

            def count_tile(jt, cnt):
                return fold(cnt, jnp.where(half_ref[jt] >= cand16, one16, zero16))

            cnt = lax.fori_loop(0, n_kt, count_tile, jnp.zeros((CNT_ROWS, tq), I16))
            return jnp.where(total(cnt) >= need, cand, base)

        return lax.fori_loop(0, 16, bit_step, jnp.full((1, tq), I16_MIN, I32))

    hi_thr = search16(hi_ref, k_sel)
    hi_thr16 = hi_thr.astype(I16)

    def bucket_tile(jt, cnt):
        hi = hi_ref[jt]
        lo_ref[jt] = jnp.where(hi == hi_thr16, lo_ref[jt], jnp.int16(I16_MIN))
        return fold(cnt, jnp.where(hi > hi_thr16, one16, zero16))

    above = total(lax.fori_loop(0, n_kt, bucket_tile, jnp.zeros((CNT_ROWS, tq), I16)))
    lo_thr = search16(lo_ref, k_sel - above)
    thr = jnp.maximum(hi_thr * 65536 + (lo_thr + 32768), INT_MIN + 1)

    m_ref[...] = jnp.full(m_ref.shape, NEG_BIG, F32)
    acc_ref[...] = jnp.zeros(acc_ref.shape, F32)

    def scores(buf_ref, jt):
        sel = key_ref[jt] >= thr
        sel2 = jnp.concatenate([sel, sel], axis=1)
        off = pl.multiple_of(jt * tk, tk)
        for p in range(n_pair):
            g = p // (n_pair // ATTN_KV_HEADS)
            kg = k_ref[pl.ds(off, tk), g * ATTN_HEAD_DIM:(g + 1) * ATTN_HEAD_DIM]
            s = _dot_nt(kg, qs_ref[2 * p * tq:(2 * p + 2) * tq, :])
            buf_ref[p] = jnp.where(sel2, s, NEG_BIG)

    def accumulate(buf_ref, jt):
        for p in range(n_pair):
            g = p // (n_pair // ATTN_KV_HEADS)
            vtg = vt_ref[jt, g * VT_ROWS:(g + 1) * VT_ROWS, :]
            m_prev = m_ref[p]
            m_cur = jnp.maximum(m_prev, _col_reduce(buf_ref[p], jnp.maximum))
            alpha = jnp.exp2(m_prev - m_cur)
            pr = jnp.exp2(buf_ref[p] - m_cur).astype(BF16)
            acc_ref[p] = acc_ref[p] * alpha + _dot(vtg, pr)
            m_ref[p] = m_cur

    scores(s_ref, 0)

    def attn_two_tiles(i, c):
        j0 = 2 * i
        scores(s2_ref, jnp.minimum(j0 + 1, n_kt - 1))
        accumulate(s_ref, j0)

        @pl.when(j0 + 1 < n_kt)
        def _():
            scores(s_ref, jnp.minimum(j0 + 2, n_kt - 1))
            accumulate(s2_ref, j0 + 1)

        return c

    lax.fori_loop(0, (n_kt + 1) // 2, attn_two_tiles, 0)
    for p in range(n_pair):
        acc = acc_ref[p]
        o = acc[:ATTN_HEAD_DIM] / acc[ATTN_HEAD_DIM:ATTN_HEAD_DIM + 1]
        for r in range(2):
            h = 2 * p + r
            o_ref[:, h * ATTN_HEAD_DIM:(h + 1) * ATTN_HEAD_DIM] = o[:, r * tq:(r + 1) * tq].T.astype(BF16)


def _dsa(q, qi, wt, k, vt, ki, *, tq=128):
    s = q.shape[0]
    tk = vt.shape[2]
    k_sel = min(TOPK_MAX, s // 4)
    whole = lambda shape: pl.BlockSpec(shape, lambda i: (0,) * len(shape), pipeline_mode=pl.Buffered(1))
    return pl.pallas_call(
        functools.partial(_dsa_body, tq=tq, tk=tk, k_sel=k_sel),
        name="dsa",
        grid=(s // tq,),
        in_specs=[
            pl.BlockSpec((tq, ATTN_WIDTH), lambda i: (i, 0)),
            pl.BlockSpec((tq, 512), lambda i: (i, 0)),
            pl.BlockSpec((IDX_HEADS, tq), lambda i: (0, i)),
            whole((s, KV_WIDTH)),
            whole((s // tk, ATTN_KV_HEADS * VT_ROWS, tk)),
            whole((s, IDX_HEAD_DIM)),
        ],
        out_specs=pl.BlockSpec((tq, ATTN_WIDTH), lambda i: (i, 0)),
        out_shape=jax.ShapeDtypeStruct((s, ATTN_WIDTH), BF16),
        scratch_shapes=[
            pltpu.VMEM((s // tk, tk, tq), I32),
            pltpu.VMEM((s // tk, tk, tq), I16),
            pltpu.VMEM((s // tk, tk, tq), I16),
            pltpu.VMEM((ATTN_HEADS * tq, ATTN_HEAD_DIM), BF16),
            pltpu.VMEM((IDX_HEADS * tq, IDX_HEAD_DIM), BF16),
            pltpu.VMEM((ATTN_HEADS // 2, tk, 2 * tq), F32),
            pltpu.VMEM((ATTN_HEADS // 2, tk, 2 * tq), F32),
            pltpu.VMEM((ATTN_HEADS // 2, 1, 2 * tq), F32),
            pltpu.VMEM((ATTN_HEADS // 2, VT_ROWS, 2 * tq), F32),
        ],
        compiler_params=pltpu.CompilerParams(
            dimension_semantics=("parallel",), vmem_limit_bytes=VMEM_LIMIT),
    )(q, qi, wt, k, vt, ki)


def _token_shift(x, prev8, first):
    rolled = pltpu.roll(x, 1, 0)
    prev_row = jnp.where(first, 0.0, prev8[7:8, :])
    row = lax.broadcasted_iota(I32, x.shape, 0)
    return jnp.where(row == 0, jnp.broadcast_to(prev_row, x.shape), rolled)


def _rwkv_prep_body(rr_ref, rk_ref, rv_ref, ms_ref, prr_ref, prk_ref, prv_ref, pms_ref,
                    mur_ref, muk_ref, muv_ref, mum_ref, w0_ref, w2_ref, a0_ref, a2_ref, g2_ref,
                    kk_ref, ka_ref, rkk_ref, bd_ref, tri_ref, blk_ref,
                    at_ref, rt_ref, bt_ref, kt_ref, v_ref, bdt_ref, kdt_ref, pct_ref, bonus_ref, g_ref):
    first = pl.program_id(0) == 0

    def lerp(x_ref, p_ref, mu_ref):
        x = x_ref[...]
        return x + (_token_shift(x, p_ref[...], first) - x) * mu_ref[...]

    r = lerp(rr_ref, prr_ref, mur_ref)
    k = lerp(rk_ref, prk_ref, muk_ref)
    v = lerp(rv_ref, prv_ref, muv_ref)
    ms = lerp(ms_ref, pms_ref, mum_ref)
    xw = ms[:, :LANES]
    xag = ms[:, LANES:]

    w = w0_ref[...] + _dot(jnp.tanh(xw).astype(BF16), w2_ref[...])
    z = -w
    softplus = jnp.maximum(z, 0.0) + jnp.log1p(jnp.exp(-jnp.abs(z)))
    lw = -jnp.exp(-softplus - 0.5)
    a_lr = jax.nn.sigmoid(a0_ref[...] + _dot(xag.astype(BF16), a2_ref[...]))
    g_ref[...] = _dot(jax.nn.sigmoid(xag).astype(BF16), g2_ref[...])

    bd = bd_ref[...]
    kk = k * kk_ref[...]
    nrm = jnp.sqrt(_dot_exact_rhs(kk * kk, bd))
    kk = kk / jnp.maximum(nrm, 1e-12)
    k = k * (1.0 + (a_lr - 1.0) * ka_ref[...])
    bonus_ref[...] = _dot_exact_rhs(r * k * rkk_ref[...], bd) * v

    cum = _dot_exact_lhs(tri_ref[...], lw)
    tot = _dot_exact_lhs(blk_ref[...], lw)
    e_neg = jnp.exp(-cum)
    e_dec = jnp.exp(tot - cum)
    b = kk * a_lr
    at_ref[...] = -kk * jnp.exp(cum - lw)
    rt_ref[...] = r * jnp.exp(cum)
    bt_ref[...] = b * e_neg
    kt_ref[...] = k * e_neg
    v_ref[...] = v
    bdt_ref[...] = (b * e_dec).T
    kdt_ref[...] = (k * e_dec).T
    pct_ref[...] = jnp.exp(tot).T


def _rwkv_prep(u, p, *, tm=512):
    s = u.shape[0]
    w = RWKV_WIDTH
    t8 = tm // 8
    tile = lambda col: pl.BlockSpec((tm, w), lambda i: (i, col // w))
    prev = lambda col: pl.BlockSpec((8, w), lambda i: (jnp.maximum(i * t8 - 1, 0), col // w))
    vec = lambda n: pl.BlockSpec((1, n), lambda i: (0, 0))
    mat = lambda a, b: pl.BlockSpec((a, b), lambda i: (0, 0))
    tm_spec = pl.BlockSpec((tm, w), lambda i: (i, 0))
    cm_spec = pl.BlockSpec((w, tm), lambda i: (0, i))
    outs = pl.pallas_call(
        _rwkv_prep_body,
        name="rwkv_prep",
        grid=(s // tm,),
        in_specs=[
            tile(U_RR), tile(U_RK), tile(U_RV),
            pl.BlockSpec((tm, 256), lambda i: (i, U_MISC // 256)),
            prev(U_RR), prev(U_RK), prev(U_RV),
            pl.BlockSpec((8, 256), lambda i: (jnp.maximum(i * t8 - 1, 0), U_MISC // 256)),
            vec(w), vec(w), vec(w), vec(256),
            vec(w), mat(LANES, w), vec(w), mat(LANES, w), mat(LANES, w),
            vec(w), vec(w), vec(w),
            mat(w, w), mat(tm, tm), mat(tm, tm),
        ],
        out_specs=[tm_spec] * 5 + [cm_spec] * 3 + [tm_spec] * 2,
        out_shape=[jax.ShapeDtypeStruct((s, w), F32)] * 5
        + [jax.ShapeDtypeStruct((w, s), F32)] * 3
        + [jax.ShapeDtypeStruct((s, w), F32)] * 2,
        compiler_params=pltpu.CompilerParams(
            dimension_semantics=("parallel",), vmem_limit_bytes=VMEM_LIMIT),
    )(u, u, u, u, u, u, u, u,
      p["mu_r"], p["mu_k"], p["mu_v"], p["mu_m"], p["w0"], p["w2"], p["a0"], p["a2"], p["g2"],
      p["k_k"], p["k_a"], p["r_k"], p["bd"], p["tri"], p["blk"])
    return outs


def _mm1(a, b):
    return _dot(a.astype(BF16), b.astype(BF16))


def _rwkv_chunk_body(at_ref, rt_ref, bt_ref, kt_ref, v_ref, bdt_ref, kdt_ref, pct_ref,
                     mc_ref, nc_ref, yc_ref, y0_ref, *, heads, chunks):
    n = RWKV_HEAD_DIM
    cl = RW_CHUNK
    ri = lax.broadcasted_iota(I32, (cl, cl), 0)
    ci = lax.broadcasted_iota(I32, (cl, cl), 1)
    strict = ri > ci
    incl = ri >= ci
    same16 = (ri // 16) == (ci // 16)
    eye = (ri == ci).astype(F32)

    pairs = [(slice(cc * cl, (cc + 1) * cl), slice(hh * n, (hh + 1) * n))
             for hh in range(heads) for cc in range(chunks)]
    each = lambda fn, *lists: [fn(*xs) for xs in zip(*lists)]

    at = [at_ref[ts, ls] for ts, ls in pairs]
    rt = [rt_ref[ts, ls] for ts, ls in pairs]
    v = [v_ref[ts, ls] for ts, ls in pairs]
    g = [_mm3_nt(jnp.concatenate([a, r], axis=0), jnp.concatenate([bt_ref[ts, ls], kt_ref[ts, ls]], axis=0))
         for a, r, (ts, ls) in zip(at, rt, pairs)]
    l = [jnp.where(strict, x[:cl, :cl], 0.0) for x in g]
    a_ak = [jnp.where(strict, x[:cl, cl:], 0.0) for x in g]
    a_rb = [jnp.where(incl, x[cl:, :cl], 0.0) for x in g]
    a_rk = [jnp.where(incl, x[cl:, cl:], 0.0) for x in g]

    ld = [jnp.where(same16, x, 0.0) for x in l]
    lo = each(lambda x, y: x - y, l, ld)
    d = [eye + x for x in ld]
    pw = ld
    for _ in range(3):
        pw = each(_mm1, pw, pw)
        d = each(lambda x, y: x + _mm1(x, y), d, pw)
    m = each(_mm1, d, lo)
    m2 = each(_mm1, m, m)
    f = each(lambda x, y: x + _mm1(y, x), d, m2)
    t = each(lambda x, y: x + _mm1(y, x), f, m)

    w1 = each(_mm1, a_ak, v)
    x = each(lambda tt, a, w: _mm1(tt, jnp.concatenate([a, w], axis=1)), t, at, w1)
    big = [_mm1(jnp.concatenate([bdt_ref[ls, ts], arb], axis=0), xx)
           for arb, xx, (ts, ls) in zip(a_rb, x, pairs)]
    kv = [_mm1(jnp.concatenate([kdt_ref[ls, ts], ark], axis=0), vv)
          for ark, vv, (ts, ls) in zip(a_rk, v, pairs)]
    for i, (ts, ls) in enumerate(pairs):
        mc_ref[ts, ls] = eye * pct_ref[ls, ts] + big[i][:n, :n]
        nc_ref[ts, ls] = big[i][:n, n:] + kv[i][:n]
        yc_ref[ts, ls] = rt[i] + big[i][n:, :n]
        y0_ref[ts, ls] = big[i][n:, n:] + kv[i][n:]


def _rwkv_chunk(at, rt, bt, kt, v, bdt, kdt, pct, *, rows=512, lanes=256):
    s, w = at.shape
    tmaj = pl.BlockSpec((rows, lanes), lambda c, h: (c, h))
    cmaj = pl.BlockSpec((lanes, rows), lambda c, h: (h, c))
    out = jax.ShapeDtypeStruct((s, w), F32)
    return pl.pallas_call(
        functools.partial(_rwkv_chunk_body, heads=lanes // RWKV_HEAD_DIM, chunks=rows // RW_CHUNK),
        name="rwkv_chunk",
        grid=(s // rows, w // lanes),
        in_specs=[tmaj] * 5 + [cmaj] * 3,
        out_specs=[tmaj] * 4,
        out_shape=[out] * 4,
        compiler_params=pltpu.CompilerParams(
            dimension_semantics=("parallel", "parallel"), vmem_limit_bytes=VMEM_LIMIT),
    )(at, rt, bt, kt, v, bdt, kdt, pct)


def _rwkv_state_body(mc_ref, nc_ref, yc_ref, y0_ref, y_ref, st_ref, *, chunks):
    n = RWKV_HEAD_DIM
    cl = RW_CHUNK

    @pl.when(pl.program_id(0) == 0)
    def _():
        st_ref[...] = jnp.zeros_like(st_ref)

    lanes = [slice(hh * n, (hh + 1) * n) for hh in range(RWKV_HEADS)]
    st = [st_ref[:, ls] for ls in lanes]
    for cc in range(chunks):
        ts = slice(cc * cl, (cc + 1) * cl)
        r = [_mm3(jnp.concatenate([mc_ref[ts, ls], yc_ref[ts, ls]], axis=0), s)
             for ls, s in zip(lanes, st)]
        for ls, x in zip(lanes, r):
            y_ref[ts, ls] = x[n:] + y0_ref[ts, ls]
        st = [x[:n] + nc_ref[ts, ls] for ls, x in zip(lanes, r)]
    for ls, s in zip(lanes, st):
        st_ref[:, ls] = s


def _rwkv_state(mc, nc, yc, y0, *, chunks=4):
    s, w = mc.shape
    blk = pl.BlockSpec((chunks * RW_CHUNK, w), lambda c: (c, 0))
    return pl.pallas_call(
        functools.partial(_rwkv_state_body, chunks=chunks),
        name="rwkv_state",
        grid=(s // (chunks * RW_CHUNK),),
        in_specs=[blk] * 4,
        out_specs=blk,
        out_shape=jax.ShapeDtypeStruct((s, w), F32),
        scratch_shapes=[pltpu.VMEM((RWKV_HEAD_DIM, w), F32)],
        compiler_params=pltpu.CompilerParams(
            dimension_semantics=("arbitrary",), vmem_limit_bytes=VMEM_LIMIT),
    )(mc, nc, yc, y0)


def _outproj_body(h_ref, attn_ref, y_ref, bonus_ref, g_ref, gain_ref, bias_ref, bd_ref, wa_ref, wr_ref, o_ref):
    y = y_ref[...]
    bd = bd_ref[...]
    inv_n = 1.0 / RWKV_HEAD_DIM
    mean = _dot_exact_rhs(y, bd) * inv_n
    yc = y - mean
    var = _dot_exact_rhs(yc * yc, bd) * inv_n
    yn = yc * lax.rsqrt(var + GN_EPS) * gain_ref[...] + bias_ref[...]
    rw = ((yn + bonus_ref[...]) * g_ref[...]).astype(BF16)
    o_ref[...] = h_ref[...] + _dot(attn_ref[...], wa_ref[...]) + _dot(rw, wr_ref[...])


def _outproj(h, attn, y, bonus, g, gain, bias, bd, wa, wr, *, tm=256):
    s, d = h.shape
    w = RWKV_WIDTH
    row = lambda n: pl.BlockSpec((tm, n), lambda i: (i, 0))
    whole = lambda a, b: pl.BlockSpec((a, b), lambda i: (0, 0))
    return pl.pallas_call(
        _outproj_body,
        name="outproj",
        grid=(s // tm,),
        in_specs=[row(d), row(w), row(w), row(w), row(w), whole(1, w), whole(1, w), whole(w, w),
                  whole(ATTN_WIDTH, d), whole(w, d)],
        out_specs=row(d),
        out_shape=jax.ShapeDtypeStruct((s, d), F32),
        compiler_params=pltpu.CompilerParams(
            dimension_semantics=("parallel",), vmem_limit_bytes=VMEM_LIMIT),
    )(h, attn, y, bonus, g, gain, bias, bd, wa, wr)


def _relayout_w_in(w_in):
    d = w_in.shape[0]
    o = np.cumsum([0, 1024, 256, 256, 512, 64, 8, 1024, 1024, 1024, 64, 64, 64])
    seg = lambda i: w_in[:, o[i]:o[i + 1]]
    q, k, v, qi, ki, wi, rr, rk, rv, wd, ad, gd = (seg(i) for i in range(12))
    z = lambda n: jnp.zeros((d, n), w_in.dtype)
    cols = [q, k, v, qi, rr, rk, rv, ki, wd, ad, gd, wi, z(LANES - 8), z(LANES)]
    return jnp.concatenate(cols, axis=1).astype(BF16)


def _pad_rows(w2, lo):
    z = jnp.zeros_like(w2)
    return (jnp.concatenate([w2, z], axis=0) if lo == 0 else jnp.concatenate([z, w2], axis=0)).astype(BF16)


def kernel(x, ffn1_norm, ffn1_w_gate, ffn1_w_up, ffn1_w_down, mix_norm, w_in, w_out, rwkv_mu_r, rwkv_mu_k, rwkv_mu_v, rwkv_mu_w, rwkv_mu_a, rwkv_mu_g, rwkv_w0, rwkv_w2, rwkv_a0, rwkv_a2, rwkv_g2, rwkv_k_k, rwkv_k_a, rwkv_r_k, rwkv_gn_gain, rwkv_gn_bias, ffn2_norm, ffn2_w_gate, ffn2_w_up, ffn2_w_down, final_norm):
    b, s, d = x.shape
    assert b == 1 and d == D_MODEL and ffn1_norm.shape[0] == 1
    h = x[0]
    row = lambda a: a.reshape(1, -1).astype(F32)
    z64 = jnp.zeros((1, LORA), F32)

    tm_prep = 256
    tix = np.arange(tm_prep)
    same_chunk = (tix[:, None] // RW_CHUNK) == (tix[None, :] // RW_CHUNK)
    hix = np.arange(RWKV_WIDTH) // RWKV_HEAD_DIM
    prep = dict(
        mu_r=row(rwkv_mu_r[0]), mu_k=row(rwkv_mu_k[0]), mu_v=row(rwkv_mu_v[0]),
        mu_m=jnp.concatenate([z64, row(rwkv_mu_w[0]), row(rwkv_mu_a[0]), row(rwkv_mu_g[0])], axis=1),
        w0=row(rwkv_w0[0]), w2=_pad_rows(rwkv_w2[0], LORA),
        a0=row(rwkv_a0[0]), a2=_pad_rows(rwkv_a2[0], 0), g2=_pad_rows(rwkv_g2[0], LORA),
        k_k=row(rwkv_k_k[0]), k_a=row(rwkv_k_a[0]), r_k=row(rwkv_r_k[0]),
        bd=jnp.asarray(hix[:, None] == hix[None, :], BF16),
        tri=jnp.asarray(same_chunk & (tix[:, None] >= tix[None, :]), BF16),
        blk=jnp.asarray(same_chunk, BF16),
    )

    h1 = _ffn(h, row(ffn1_norm[0]), ffn1_w_gate[0].astype(BF16), ffn1_w_up[0].astype(BF16),
              ffn1_w_down[0].astype(BF16), row(final_norm), final_norm=False)
    u = _inproj(h1, row(mix_norm[0]), _relayout_w_in(w_in[0]))

    q, k, vt, qi, ki, wt = _attn_prep(u, _rope_tables(s, ATTN_HEAD_DIM), _rope_tables(s, IDX_HEAD_DIM),
                                     tm=DSA_KEY_TILE)
    attn = _dsa(q, qi, wt, k, vt, ki)

    at, rt, bt, kt, rv, bdt, kdt, pct, bonus, g = _rwkv_prep(u, prep, tm=tm_prep)
    y = _rwkv_state(*_rwkv_chunk(at, rt, bt, kt, rv, bdt, kdt, pct))

    wo = w_out[0].astype(BF16)
    h2 = _outproj(h1, attn, y, bonus, g, row(rwkv_gn_gain[0]), row(rwkv_gn_bias[0]), prep["bd"],
                  wo[:ATTN_WIDTH], wo[ATTN_WIDTH:])
    out = _ffn(h2, row(ffn2_norm[0]), ffn2_w_gate[0].astype(BF16), ffn2_w_up[0].astype(BF16),
               ffn2_w_down[0].astype(BF16), row(final_norm), final_norm=True)
    return out[None]
```

```python
import functools

import jax
import jax.numpy as jnp
import numpy as np
from jax import lax
from jax.experimental import pallas as pl
from jax.experimental.pallas import tpu as pltpu

F32 = jnp.float32
BF16 = jnp.bfloat16
I32 = jnp.int32

D_MODEL = 2048
CHUNK = 64
ROPE_THETA = 500000.0
ROPE_FRACTION = 4
NORM_EPS = 1e-6
ATTN_HEAD_DIM = 128
ATTN_WIDTH = 1024
ATTN_HEADS = 8
ATTN_KV_HEADS = 2
KV_WIDTH = 256
IDX_HEADS = 8
IDX_HEAD_DIM = 64
TOPK_MAX = 256
RWKV_HEAD_DIM = 64
RWKV_WIDTH = 1024
RWKV_HEADS = 16
LORA = 64
GN_EPS = 64e-5
D_FF = 5632

LANES = 128
VMEM_LIMIT = 56 * 1024 * 1024

U_WIDTH = 5632
U_Q = 0
U_KV = 1024
U_QI = 1536
U_RR = 2048
U_RK = 3072
U_RV = 4096
U_MISC = 5120

RW_CHUNK = 64
DSA_KEY_TILE = 512
VT_ROWS = ATTN_HEAD_DIM + 16
CNT_ROWS = 128
I16 = jnp.int16
I16_MIN = -2 ** 15
LOG2E = 1.4426950408889634
INT_MIN = -2 ** 31
NEG_BIG = -1e30


def _rms(x, g):
    return x * lax.rsqrt(jnp.mean(x * x, axis=-1, keepdims=True) + NORM_EPS) * g


def _dot(a, b):
    return jnp.dot(a, b, preferred_element_type=F32)


def _dot_nt(a, b):
    return lax.dot_general(a, b, (((1,), (1,)), ((), ())), preferred_element_type=F32)


def _split3(x):
    hi = x.astype(BF16)
    r1 = x - hi.astype(F32)
    mid = r1.astype(BF16)
    lo = (r1 - mid.astype(F32)).astype(BF16)
    return hi, mid, lo


def _dot_seg(x, m):
    hi, lo = _split2(x)
    return _dot(hi, m) + _dot(lo, m)


def _dot_exact_lhs(m, x):
    hi, mid, lo = _split3(x)
    return _dot(m, hi) + _dot(m, mid) + _dot(m, lo)


def _split2(x):
    hi = x.astype(BF16)
    lo = (x - hi.astype(F32)).astype(BF16)
    return hi, lo


def _mm3(a, b):
    ah, al = _split2(a)
    bh, bl = _split2(b)
    return _dot(ah, bh) + _dot(ah, bl) + _dot(al, bh)


def _mm3_nt(a, b):
    ah, al = _split2(a)
    bh, bl = _split2(b)
    return _dot_nt(ah, bh) + _dot_nt(ah, bl) + _dot_nt(al, bh)


def _ffn_body(x_ref, g_ref, wg_ref, wu_ref, wd_ref, fg_ref, o_ref, xn_ref, acc_ref, *, final_norm):
    j = pl.program_id(1)

    @pl.when(j == 0)
    def _():
        xn_ref[...] = _rms(x_ref[...], g_ref[...]).astype(BF16)
        acc_ref[...] = jnp.zeros_like(acc_ref)

    xn = xn_ref[...]
    gt = _dot(xn, wg_ref[...])
    ut = _dot(xn, wu_ref[...])
    act = (gt * jax.nn.sigmoid(gt) * ut).astype(BF16)
    acc_ref[...] += _dot(act, wd_ref[...])

    @pl.when(j == pl.num_programs(1) - 1)
    def _():
        h = x_ref[...] + 0.5 * acc_ref[...]
        if final_norm:
            h = _rms(h, fg_ref[...])
        o_ref[...] = h


def _ffn(x, gain, wg, wu, wd, fgain, *, final_norm, tm=512, tf=512):
    s, d = x.shape
    f = wg.shape[1]
    return pl.pallas_call(
        functools.partial(_ffn_body, final_norm=final_norm),
        name="ffn_final" if final_norm else "ffn",
        grid=(s // tm, f // tf),
        in_specs=[
            pl.BlockSpec((tm, d), lambda i, j: (i, 0)),
            pl.BlockSpec((1, d), lambda i, j: (0, 0)),
            pl.BlockSpec((d, tf), lambda i, j: (0, j)),
            pl.BlockSpec((d, tf), lambda i, j: (0, j)),
            pl.BlockSpec((tf, d), lambda i, j: (j, 0)),
            pl.BlockSpec((1, d), lambda i, j: (0, 0)),
        ],
        out_specs=pl.BlockSpec((tm, d), lambda i, j: (i, 0)),
        out_shape=jax.ShapeDtypeStruct((s, d), F32),
        scratch_shapes=[pltpu.VMEM((tm, d), BF16), pltpu.VMEM((tm, d), F32)],
        compiler_params=pltpu.CompilerParams(
            dimension_semantics=("parallel", "arbitrary"), vmem_limit_bytes=VMEM_LIMIT),
    )(x, gain, wg, wu, wd, fgain)


def _inproj_body(x_ref, g_ref, w_ref, o_ref, xn_ref):
    @pl.when(pl.program_id(1) == 0)
    def _():
        xn_ref[...] = _rms(x_ref[...], g_ref[...]).astype(BF16)

    o_ref[...] = _dot(xn_ref[...], w_ref[...])


def _inproj(h, gain, w, *, tm=1024, tn=512):
    s, d = h.shape
    n = w.shape[1]
    return pl.pallas_call(
        _inproj_body,
        name="inproj",
        grid=(s // tm, n // tn),
        in_specs=[
            pl.BlockSpec((tm, d), lambda i, j: (i, 0)),
            pl.BlockSpec((1, d), lambda i, j: (0, 0)),
            pl.BlockSpec((d, tn), lambda i, j: (0, j)),
        ],
        out_specs=pl.BlockSpec((tm, tn), lambda i, j: (i, j)),
        out_shape=jax.ShapeDtypeStruct((s, n), F32),
        scratch_shapes=[pltpu.VMEM((tm, d), BF16)],
        compiler_params=pltpu.CompilerParams(
            dimension_semantics=("parallel", "arbitrary"), vmem_limit_bytes=VMEM_LIMIT),
    )(h, gain, w)


def _rope_tables(s, head_dim):
    rot = head_dim // ROPE_FRACTION
    half = rot // 2
    inv_freq = ROPE_THETA ** (-(jnp.arange(half, dtype=F32) * 2.0 / rot))
    ang = jnp.arange(s, dtype=F32)[:, None] * inv_freq[None, :]
    cos, sin = jnp.cos(ang), jnp.sin(ang)
    ones = jnp.ones((s, head_dim - rot), F32)
    zeros = jnp.zeros((s, head_dim - rot), F32)
    zh = jnp.zeros((s, half), F32)
    c = jnp.concatenate([cos, cos, ones], axis=1)
    sp = jnp.concatenate([zh, sin, zeros], axis=1)
    sm = jnp.concatenate([-sin, zh, zeros], axis=1)
    reps = LANES // head_dim
    return tuple(jnp.tile(t, (1, reps)) for t in (c, sp, sm))


def _rope128(x, c, sp, sm, half):
    return x * c + pltpu.roll(x, half, 1) * sp + pltpu.roll(x, LANES - half, 1) * sm


def _attn_prep_body(q_ref, kv_ref, qi_ref, misc_ref, c1_ref, sp1_ref, sm1_ref, c2_ref, sp2_ref, sm2_ref,
                    qo_ref, ko_ref, vo_ref, qio_ref, kio_ref, wio_ref):
    c1, sp1, sm1 = c1_ref[...], sp1_ref[...], sm1_ref[...]
    c2, sp2, sm2 = c2_ref[...], sp2_ref[...], sm2_ref[...]
    att_scale = ATTN_HEAD_DIM ** -0.5 * LOG2E
    for b in range(ATTN_WIDTH // LANES):
        sl = slice(b * LANES, (b + 1) * LANES)
        qo_ref[:, sl] = (_rope128(q_ref[:, sl], c1, sp1, sm1, 16) * att_scale).astype(BF16)
    for b in range(KV_WIDTH // LANES):
        sl = slice(b * LANES, (b + 1) * LANES)
        ko_ref[:, sl] = _rope128(kv_ref[:, sl], c1, sp1, sm1, 16).astype(BF16)
    vt = kv_ref[:, KV_WIDTH:].T.astype(BF16)
    ones = jnp.ones((VT_ROWS - ATTN_HEAD_DIM, vt.shape[1]), BF16)
    for g in range(ATTN_KV_HEADS):
        vo_ref[0, g * VT_ROWS:g * VT_ROWS + ATTN_HEAD_DIM, :] = vt[g * ATTN_HEAD_DIM:(g + 1) * ATTN_HEAD_DIM]
        vo_ref[0, g * VT_ROWS + ATTN_HEAD_DIM:(g + 1) * VT_ROWS, :] = ones
    for b in range(IDX_HEADS * IDX_HEAD_DIM // LANES):
        sl = slice(b * LANES, (b + 1) * LANES)
        qio_ref[:, sl] = _rope128(qi_ref[:, sl], c2, sp2, sm2, 8).astype(BF16)
    ki = _rope128(misc_ref[:, :LANES], c2, sp2, sm2, 8)
    kio_ref[...] = ki[:, :IDX_HEAD_DIM].astype(BF16)
    idx_scale = (IDX_HEADS ** -0.5) * (IDX_HEAD_DIM ** -0.5)
    wio_ref[...] = (misc_ref[:, 2 * LANES:3 * LANES] * idx_scale).T[:IDX_HEADS]


def _attn_prep(u, tabs128, tabs64, *, tm):
    s = u.shape[0]
    tab_spec = pl.BlockSpec((tm, LANES), lambda i: (i, 0))
    return pl.pallas_call(
        _attn_prep_body,
        name="attn_prep",
        grid=(s // tm,),
        in_specs=[
            pl.BlockSpec((tm, ATTN_WIDTH), lambda i: (i, U_Q // ATTN_WIDTH)),
            pl.BlockSpec((tm, 512), lambda i: (i, U_KV // 512)),
            pl.BlockSpec((tm, 512), lambda i: (i, U_QI // 512)),
            pl.BlockSpec((tm, 512), lambda i: (i, U_MISC // 512)),
        ] + [tab_spec] * 6,
        out_specs=[
            pl.BlockSpec((tm, ATTN_WIDTH), lambda i: (i, 0)),
            pl.BlockSpec((tm, KV_WIDTH), lambda i: (i, 0)),
            pl.BlockSpec((1, ATTN_KV_HEADS * VT_ROWS, tm), lambda i: (i, 0, 0)),
            pl.BlockSpec((tm, 512), lambda i: (i, 0)),
            pl.BlockSpec((tm, IDX_HEAD_DIM), lambda i: (i, 0)),
            pl.BlockSpec((IDX_HEADS, tm), lambda i: (0, i)),
        ],
        out_shape=[
            jax.ShapeDtypeStruct((s, ATTN_WIDTH), BF16),
            jax.ShapeDtypeStruct((s, KV_WIDTH), BF16),
            jax.ShapeDtypeStruct((s // tm, ATTN_KV_HEADS * VT_ROWS, tm), BF16),
            jax.ShapeDtypeStruct((s, 512), BF16),
            jax.ShapeDtypeStruct((s, IDX_HEAD_DIM), BF16),
            jax.ShapeDtypeStruct((IDX_HEADS, s), F32),
        ],
        compiler_params=pltpu.CompilerParams(
            dimension_semantics=("parallel",), vmem_limit_bytes=VMEM_LIMIT),
    )(u, u, u, u, *tabs128, *tabs64)


def _col_reduce(x, op):
    n, w = x.shape
    y = x.reshape(n // 64, 64, w)
    acc = y[0]
    for a in range(1, n // 64):
        acc = op(acc, y[a])
    z = acc.reshape(8, 8, w)
    t = [op(z[2 * i], z[2 * i + 1]) for i in range(4)]
    r = op(op(t[0], t[1]), op(t[2], t[3]))
    red = jnp.max if op is jnp.maximum else jnp.sum
    return red(r, axis=0, keepdims=True)


def _dsa_body(q_ref, qi_ref, wt_ref, k_ref, vt_ref, ki_ref, o_ref,
              key_ref, hi_ref, lo_ref, qs_ref, qis_ref, s_ref, s2_ref, smax_ref, smax2_ref, m_ref, acc_ref, *, tq, tk, k_sel):
    qb = pl.program_id(0)
    q_end = (qb + 1) * tq
    n_kt = (q_end + tk - 1) // tk
    n_full = (qb * tq) // tk
    n_pair = ATTN_HEADS // 2

    for h in range(ATTN_HEADS):
        qs_ref[h * tq:(h + 1) * tq, :] = q_ref[:, h * ATTN_HEAD_DIM:(h + 1) * ATTN_HEAD_DIM]
    for h in range(IDX_HEADS):
        qis_ref[h * tq:(h + 1) * tq, :] = qi_ref[:, h * IDX_HEAD_DIM:(h + 1) * IDX_HEAD_DIM]

    qpos = lax.broadcasted_iota(I32, (1, tq), 1) + qb * tq
    key_lim = (qpos // CHUNK + 1) * CHUNK
    wt = wt_ref[...]

    def score_tile(jt, masked):
        kit = ki_ref[pl.ds(pl.multiple_of(jt * tk, tk), tk), :]
        acc = jnp.zeros((tk, tq), F32)
        for p in range(IDX_HEADS // 2):
            s2 = _dot_nt(kit, qis_ref[2 * p * tq:(2 * p + 2) * tq, :])
            acc = acc + jnp.maximum(s2[:, :tq], 0.0) * wt[2 * p:2 * p + 1, :]
            acc = acc + jnp.maximum(s2[:, tq:], 0.0) * wt[2 * p + 1:2 * p + 2, :]
        bits = pltpu.bitcast(acc, I32)
        key = bits ^ ((bits >> 31) & 0x7FFFFFFF)
        if masked:
            kpos = lax.broadcasted_iota(I32, (tk, tq), 0) + jt * tk
            key = jnp.where(kpos < key_lim, key, INT_MIN)
        key_ref[jt] = key
        hi_ref[jt] = (key >> 16).astype(I16)
        lo_ref[jt] = (key ^ 0x8000).astype(I16)

    def two_full_tiles(i, c):
        score_tile(2 * i, False)
        score_tile(2 * i + 1, False)
        return c

    def diag_tile(jt, c):
        score_tile(jt, True)
        return c

    lax.fori_loop(0, n_full // 2, two_full_tiles, 0)
    lax.fori_loop(n_full - n_full % 2, n_kt, diag_tile, 0)

    @pl.when(n_kt % 2 == 1)
    def _():
        key_ref[n_kt] = jnp.full((tk, tq), INT_MIN, I32)

    one16 = jnp.ones((tk, tq), I16)
    zero16 = jnp.zeros((tk, tq), I16)

    def fold(cnt, hit):
        hit = hit.reshape(tk // CNT_ROWS, CNT_ROWS, tq)
        for a in range(tk // CNT_ROWS):
            cnt = cnt + hit[a]
        return cnt

    def total(cnt):
        return jnp.sum(cnt.astype(I32), axis=0, keepdims=True)

    def search16(half_ref, need):
        def bit_step(i, base):
            cand = base + lax.shift_left(jnp.int32(1), 15 - i)
            cand16 = cand.astype(I16)

            def count_tile(jt, cnt):
                return fold(cnt, jnp.where(half_ref[jt] >= cand16, one16, zero16))

            cnt = lax.fori_loop(0, n_kt, count_tile, jnp.zeros((CNT_ROWS, tq), I16))
            return jnp.where(total(cnt) >= need, cand, base)

        return lax.fori_loop(0, 16, bit_step, jnp.full((1, tq), I16_MIN, I32))

    hi_thr = search16(hi_ref, k_sel)
    hi_thr16 = hi_thr.astype(I16)

    def bucket_tile(jt, cnt):
        hi = hi_ref[jt]
        lo_ref[jt] = jnp.where(hi == hi_thr16, lo_ref[jt], jnp.int16(I16_MIN))
        return fold(cnt, jnp.where(hi > hi_thr16, one16, zero16))

    above = total(lax.fori_loop(0, n_kt, bucket_tile, jnp.zeros((CNT_ROWS, tq), I16)))
    lo_thr = search16(lo_ref, k_sel - above)
    thr = jnp.maximum(hi_thr * 65536 + (lo_thr + 32768), INT_MIN + 1)

    m_ref[...] = jnp.full(m_ref.shape, NEG_BIG, F32)
    acc_ref[...] = jnp.zeros(acc_ref.shape, F32)

    def scores(buf_ref, max_ref, jt):
        sel = key_ref[jt] >= thr
        sel2 = jnp.concatenate([sel, sel], axis=1)
        off = pl.multiple_of(jt * tk, tk)
        for p in range(n_pair):
            g = p // (n_pair // ATTN_KV_HEADS)
            kg = k_ref[pl.ds(off, tk), g * ATTN_HEAD_DIM:(g + 1) * ATTN_HEAD_DIM]
            s = _dot_nt(kg, qs_ref[2 * p * tq:(2 * p + 2) * tq, :])
            s = jnp.where(sel2, s, NEG_BIG)
            buf_ref[p] = s
            max_ref[p] = _col_reduce(s, jnp.maximum)

    def accumulate(buf_ref, max_ref, jt):
        for p in range(n_pair):
            g = p // (n_pair // ATTN_KV_HEADS)
            vtg = vt_ref[jt, g * VT_ROWS:(g + 1) * VT_ROWS, :]
            m_prev = m_ref[p]
            m_cur = jnp.maximum(m_prev, max_ref[p])
            alpha = jnp.exp2(m_prev - m_cur)
            pr = jnp.exp2(buf_ref[p] - m_cur).astype(BF16)
            acc_ref[p] = acc_ref[p] * alpha + _dot(vtg, pr)
            m_ref[p] = m_cur

    scores(s_ref, smax_ref, 0)

    def attn_two_tiles(i, c):
        j0 = 2 * i
        scores(s2_ref, smax2_ref, j0 + 1)
        accumulate(s_ref, smax_ref, j0)
        scores(s_ref, smax_ref, jnp.minimum(j0 + 2, n_kt - 1))
        accumulate(s2_ref, smax2_ref, j0 + 1)
        return c

    lax.fori_loop(0, (n_kt + 1) // 2, attn_two_tiles, 0)
    for p in range(n_pair):
        acc = acc_ref[p]
        o = acc[:ATTN_HEAD_DIM] / acc[ATTN_HEAD_DIM:ATTN_HEAD_DIM + 1]
        for r in range(2):
            h = 2 * p + r
            o_ref[:, h * ATTN_HEAD_DIM:(h + 1) * ATTN_HEAD_DIM] = o[:, r * tq:(r + 1) * tq].T.astype(BF16)


def _dsa(q, qi, wt, k, vt, ki, *, tq=128):
    s = q.shape[0]
    tk = vt.shape[2]
    k_sel = min(TOPK_MAX, s // 4)
    whole = lambda shape: pl.BlockSpec(shape, lambda i: (0,) * len(shape), pipeline_mode=pl.Buffered(1))
    return pl.pallas_call(
        functools.partial(_dsa_body, tq=tq, tk=tk, k_sel=k_sel),
        name="dsa",
        grid=(s // tq,),
        in_specs=[
            pl.BlockSpec((tq, ATTN_WIDTH), lambda i: (i, 0)),
            pl.BlockSpec((tq, 512), lambda i: (i, 0)),
            pl.BlockSpec((IDX_HEADS, tq), lambda i: (0, i)),
            whole((s, KV_WIDTH)),
            whole((s // tk, ATTN_KV_HEADS * VT_ROWS, tk)),
            whole((s, IDX_HEAD_DIM)),
        ],
        out_specs=pl.BlockSpec((tq, ATTN_WIDTH), lambda i: (i, 0)),
        out_shape=jax.ShapeDtypeStruct((s, ATTN_WIDTH), BF16),
        scratch_shapes=[
            pltpu.VMEM((s // tk, tk, tq), I32),
            pltpu.VMEM((s // tk, tk, tq), I16),
            pltpu.VMEM((s // tk, tk, tq), I16),
            pltpu.VMEM((ATTN_HEADS * tq, ATTN_HEAD_DIM), BF16),
            pltpu.VMEM((IDX_HEADS * tq, IDX_HEAD_DIM), BF16),
            pltpu.VMEM((ATTN_HEADS // 2, tk, 2 * tq), F32),
            pltpu.VMEM((ATTN_HEADS // 2, tk, 2 * tq), F32),
            pltpu.VMEM((ATTN_HEADS // 2, 1, 2 * tq), F32),
            pltpu.VMEM((ATTN_HEADS // 2, 1, 2 * tq), F32),
            pltpu.VMEM((ATTN_HEADS // 2, 1, 2 * tq), F32),
            pltpu.VMEM((ATTN_HEADS // 2, VT_ROWS, 2 * tq), F32),
        ],
        compiler_params=pltpu.CompilerParams(
            dimension_semantics=("parallel",), vmem_limit_bytes=VMEM_LIMIT),
    )(q, qi, wt, k, vt, ki)


def _token_shift(x, prev8, first):
    rolled = pltpu.roll(x, 1, 0)
    prev_row = jnp.where(first, 0.0, prev8[7:8, :])
    row = lax.broadcasted_iota(I32, x.shape, 0)
    return jnp.where(row == 0, jnp.broadcast_to(prev_row, x.shape), rolled)


def _rwkv_prep_body(rr_ref, rk_ref, rv_ref, ms_ref, prr_ref, prk_ref, prv_ref, pms_ref,
                    mur_ref, muk_ref, muv_ref, mum_ref, w0_ref, w2_ref, a0_ref, a2_ref, g2_ref,
                    kk_ref, ka_ref, rkk_ref, bd_ref, tri_ref, blk_ref,
                    at_ref, rt_ref, bt_ref, kt_ref, v_ref, bdt_ref, kdt_ref, pct_ref, bonus_ref, g_ref):
    first = pl.program_id(0) == 0

    def lerp(x_ref, p_ref, mu_ref):
        x = x_ref[...]
        return x + (_token_shift(x, p_ref[...], first) - x) * mu_ref[...]

    r = lerp(rr_ref, prr_ref, mur_ref)
    k = lerp(rk_ref, prk_ref, muk_ref)
    v = lerp(rv_ref, prv_ref, muv_ref)
    ms = lerp(ms_ref, pms_ref, mum_ref)
    xw = ms[:, :LANES]
    xag = ms[:, LANES:]

    w = w0_ref[...] + _dot(jnp.tanh(xw).astype(BF16), w2_ref[...])
    z = -w
    softplus = jnp.maximum(z, 0.0) + jnp.log1p(jnp.exp(-jnp.abs(z)))
    lw = -jnp.exp(-softplus - 0.5)
    a_lr = jax.nn.sigmoid(a0_ref[...] + _dot(xag.astype(BF16), a2_ref[...]))
    g_ref[...] = _dot(jax.nn.sigmoid(xag).astype(BF16), g2_ref[...])

    bd = bd_ref[...]
    kk = k * kk_ref[...]
    nrm = jnp.sqrt(_dot_seg(kk * kk, bd))
    kk = kk / jnp.maximum(nrm, 1e-12)
    k = k * (1.0 + (a_lr - 1.0) * ka_ref[...])
    bonus_ref[...] = _dot_seg(r * k * rkk_ref[...], bd) * v

    cum = _dot_exact_lhs(tri_ref[...], lw)
    tot = _dot_exact_lhs(blk_ref[...], lw)
    e_neg = jnp.exp(-cum)
    e_dec = jnp.exp(tot - cum)
    b = kk * a_lr
    at_ref[...] = -kk * jnp.exp(cum - lw)
    rt_ref[...] = r * jnp.exp(cum)
    bt_ref[...] = b * e_neg
    kt_ref[...] = k * e_neg
    v_ref[...] = v
    bdt_ref[...] = (b * e_dec).T
    kdt_ref[...] = (k * e_dec).T
    pct_ref[...] = jnp.exp(tot).T


def _rwkv_prep(u, p, *, tm=512):
    s = u.shape[0]
    w = RWKV_WIDTH
    t8 = tm // 8
    tile = lambda col: pl.BlockSpec((tm, w), lambda i: (i, col // w))
    prev = lambda col: pl.BlockSpec((8, w), lambda i: (jnp.maximum(i * t8 - 1, 0), col // w))
    vec = lambda n: pl.BlockSpec((1, n), lambda i: (0, 0))
    mat = lambda a, b: pl.BlockSpec((a, b), lambda i: (0, 0))
    tm_spec = pl.BlockSpec((tm, w), lambda i: (i, 0))
    cm_spec = pl.BlockSpec((w, tm), lambda i: (0, i))
    outs = pl.pallas_call(
        _rwkv_prep_body,
        name="rwkv_prep",
        grid=(s // tm,),
        in_specs=[
            tile(U_RR), tile(U_RK), tile(U_RV),
            pl.BlockSpec((tm, 256), lambda i: (i, U_MISC // 256)),
            prev(U_RR), prev(U_RK), prev(U_RV),
            pl.BlockSpec((8, 256), lambda i: (jnp.maximum(i * t8 - 1, 0), U_MISC // 256)),
            vec(w), vec(w), vec(w), vec(256),
            vec(w), mat(LANES, w), vec(w), mat(LANES, w), mat(LANES, w),
            vec(w), vec(w), vec(w),
            mat(w, w), mat(tm, tm), mat(tm, tm),
        ],
        out_specs=[tm_spec] * 5 + [cm_spec] * 3 + [tm_spec] * 2,
        out_shape=[jax.ShapeDtypeStruct((s, w), F32)] * 5
        + [jax.ShapeDtypeStruct((w, s), F32)] * 3
        + [jax.ShapeDtypeStruct((s, w), F32)] * 2,
        compiler_params=pltpu.CompilerParams(
            dimension_semantics=("parallel",), vmem_limit_bytes=VMEM_LIMIT),
    )(u, u, u, u, u, u, u, u,
      p["mu_r"], p["mu_k"], p["mu_v"], p["mu_m"], p["w0"], p["w2"], p["a0"], p["a2"], p["g2"],
      p["k_k"], p["k_a"], p["r_k"], p["bd"], p["tri"], p["blk"])
    return outs


def _mm1(a, b):
    return _dot(a.astype(BF16), b.astype(BF16))


def _rwkv_chunk_body(at_ref, rt_ref, bt_ref, kt_ref, v_ref, bdt_ref, kdt_ref, pct_ref,
                     mc_ref, nc_ref, yc_ref, y0_ref, *, heads, chunks):
    n = RWKV_HEAD_DIM
    cl = RW_CHUNK
    ri = lax.broadcasted_iota(I32, (cl, cl), 0)
    ci = lax.broadcasted_iota(I32, (cl, cl), 1)
    strict = ri > ci
    incl = ri >= ci
    same16 = (ri // 16) == (ci // 16)
    eye = (ri == ci).astype(F32)

    pairs = [(slice(cc * cl, (cc + 1) * cl), slice(hh * n, (hh + 1) * n))
             for hh in range(heads) for cc in range(chunks)]
    each = lambda fn, *lists: [fn(*xs) for xs in zip(*lists)]

    at = [at_ref[ts, ls] for ts, ls in pairs]
    rt = [rt_ref[ts, ls] for ts, ls in pairs]
    v = [v_ref[ts, ls] for ts, ls in pairs]
    g = [_mm3_nt(jnp.concatenate([a, r], axis=0), jnp.concatenate([bt_ref[ts, ls], kt_ref[ts, ls]], axis=0))
         for a, r, (ts, ls) in zip(at, rt, pairs)]
    l = [jnp.where(strict, x[:cl, :cl], 0.0) for x in g]
    a_ak = [jnp.where(strict, x[:cl, cl:], 0.0) for x in g]
    a_rb = [jnp.where(incl, x[cl:, :cl], 0.0) for x in g]
    a_rk = [jnp.where(incl, x[cl:, cl:], 0.0) for x in g]

    ld = [jnp.where(same16, x, 0.0) for x in l]
    lo = each(lambda x, y: x - y, l, ld)
    d = [eye + x for x in ld]
    pw = ld
    for _ in range(3):
        pw = each(_mm1, pw, pw)
        d = each(lambda x, y: x + _mm1(x, y), d, pw)
    m = each(_mm1, d, lo)
    m2 = each(_mm1, m, m)
    f = each(lambda x, y: x + _mm1(y, x), d, m2)
    t = each(lambda x, y: x + _mm1(y, x), f, m)

    w1 = each(_mm1, a_ak, v)
    x = each(lambda tt, a, w: _mm1(tt, jnp.concatenate([a, w], axis=1)), t, at, w1)
    big = [_mm1(jnp.concatenate([bdt_ref[ls, ts], arb], axis=0), xx)
           for arb, xx, (ts, ls) in zip(a_rb, x, pairs)]
    kv = [_mm1(jnp.concatenate([kdt_ref[ls, ts], ark], axis=0), vv)
          for ark, vv, (ts, ls) in zip(a_rk, v, pairs)]
    for i, (ts, ls) in enumerate(pairs):
        mc_ref[ts, ls] = eye * pct_ref[ls, ts] + big[i][:n, :n]
        nc_ref[ts, ls] = big[i][:n, n:] + kv[i][:n]
        yc_ref[ts, ls] = rt[i] + big[i][n:, :n]
        y0_ref[ts, ls] = big[i][n:, n:] + kv[i][n:]


def _rwkv_chunk(at, rt, bt, kt, v, bdt, kdt, pct, *, rows=512, lanes=256):
    s, w = at.shape
    tmaj = pl.BlockSpec((rows, lanes), lambda c, h: (c, h))
    cmaj = pl.BlockSpec((lanes, rows), lambda c, h: (h, c))
    out = jax.ShapeDtypeStruct((s, w), F32)
    return pl.pallas_call(
        functools.partial(_rwkv_chunk_body, heads=lanes // RWKV_HEAD_DIM, chunks=rows // RW_CHUNK),
        name="rwkv_chunk",
        grid=(s // rows, w // lanes),
        in_specs=[tmaj] * 5 + [cmaj] * 3,
        out_specs=[tmaj] * 4,
        out_shape=[out] * 4,
        compiler_params=pltpu.CompilerParams(
            dimension_semantics=("parallel", "parallel"), vmem_limit_bytes=VMEM_LIMIT),
    )(at, rt, bt, kt, v, bdt, kdt, pct)


def _rwkv_state_body(mc_ref, nc_ref, yc_ref, y0_ref, y_ref, st_ref, *, chunks):
    n = RWKV_HEAD_DIM
    cl = RW_CHUNK

    @pl.when(pl.program_id(0) == 0)
    def _():
        st_ref[...] = jnp.zeros_like(st_ref)

    lanes = [slice(hh * n, (hh + 1) * n) for hh in range(RWKV_HEADS)]
    st = [st_ref[:, ls] for ls in lanes]
    for cc in range(chunks):
        ts = slice(cc * cl, (cc + 1) * cl)
        r = [_mm3(jnp.concatenate([mc_ref[ts, ls], yc_ref[ts, ls]], axis=0), s)
             for ls, s in zip(lanes, st)]
        for ls, x in zip(lanes, r):
            y_ref[ts, ls] = x[n:] + y0_ref[ts, ls]
        st = [x[:n] + nc_ref[ts, ls] for ls, x in zip(lanes, r)]
    for ls, s in zip(lanes, st):
        st_ref[:, ls] = s


def _rwkv_state(mc, nc, yc, y0, *, chunks=4):
    s, w = mc.shape
    blk = pl.BlockSpec((chunks * RW_CHUNK, w), lambda c: (c, 0))
    return pl.pallas_call(
        functools.partial(_rwkv_state_body, chunks=chunks),
        name="rwkv_state",
        grid=(s // (chunks * RW_CHUNK),),
        in_specs=[blk] * 4,
        out_specs=blk,
        out_shape=jax.ShapeDtypeStruct((s, w), F32),
        scratch_shapes=[pltpu.VMEM((RWKV_HEAD_DIM, w), F32)],
        compiler_params=pltpu.CompilerParams(
            dimension_semantics=("arbitrary",), vmem_limit_bytes=VMEM_LIMIT),
    )(mc, nc, yc, y0)


def _outproj_body(h_ref, attn_ref, y_ref, bonus_ref, g_ref, gain_ref, bias_ref, bd_ref, wa_ref, wr_ref, o_ref):
    y = y_ref[...]
    bd = bd_ref[...]
    inv_n = 1.0 / RWKV_HEAD_DIM
    mean = _dot_seg(y, bd) * inv_n
    yc = y - mean
    var = _dot_seg(yc * yc, bd) * inv_n
    yn = yc * lax.rsqrt(var + GN_EPS) * gain_ref[...] + bias_ref[...]
    rw = ((yn + bonus_ref[...]) * g_ref[...]).astype(BF16)
    o_ref[...] = h_ref[...] + _dot(attn_ref[...], wa_ref[...]) + _dot(rw, wr_ref[...])


def _outproj(h, attn, y, bonus, g, gain, bias, bd, wa, wr, *, tm=256):
    s, d = h.shape
    w = RWKV_WIDTH
    row = lambda n: pl.BlockSpec((tm, n), lambda i: (i, 0))
    whole = lambda a, b: pl.BlockSpec((a, b), lambda i: (0, 0))
    return pl.pallas_call(
        _outproj_body,
        name="outproj",
        grid=(s // tm,),
        in_specs=[row(d), row(w), row(w), row(w), row(w), whole(1, w), whole(1, w), whole(w, w),
                  whole(ATTN_WIDTH, d), whole(w, d)],
        out_specs=row(d),
        out_shape=jax.ShapeDtypeStruct((s, d), F32),
        compiler_params=pltpu.CompilerParams(
            dimension_semantics=("parallel",), vmem_limit_bytes=VMEM_LIMIT),
    )(h, attn, y, bonus, g, gain, bias, bd, wa, wr)


def _relayout_w_in(w_in):
    d = w_in.shape[0]
    o = np.cumsum([0, 1024, 256, 256, 512, 64, 8, 1024, 1024, 1024, 64, 64, 64])
    seg = lambda i: w_in[:, o[i]:o[i + 1]]
    q, k, v, qi, ki, wi, rr, rk, rv, wd, ad, gd = (seg(i) for i in range(12))
    z = lambda n: jnp.zeros((d, n), w_in.dtype)
    cols = [q, k, v, qi, rr, rk, rv, ki, wd, ad, gd, wi, z(LANES - 8), z(LANES)]
    return jnp.concatenate(cols, axis=1).astype(BF16)


def _pad_rows(w2, lo):
    z = jnp.zeros_like(w2)
    return (jnp.concatenate([w2, z], axis=0) if lo == 0 else jnp.concatenate([z, w2], axis=0)).astype(BF16)


def kernel(x, ffn1_norm, ffn1_w_gate, ffn1_w_up, ffn1_w_down, mix_norm, w_in, w_out, rwkv_mu_r, rwkv_mu_k, rwkv_mu_v, rwkv_mu_w, rwkv_mu_a, rwkv_mu_g, rwkv_w0, rwkv_w2, rwkv_a0, rwkv_a2, rwkv_g2, rwkv_k_k, rwkv_k_a, rwkv_r_k, rwkv_gn_gain, rwkv_gn_bias, ffn2_norm, ffn2_w_gate, ffn2_w_up, ffn2_w_down, final_norm):
    b, s, d = x.shape
    assert b == 1 and d == D_MODEL and ffn1_norm.shape[0] == 1
    h = x[0]
    row = lambda a: a.reshape(1, -1).astype(F32)
    z64 = jnp.zeros((1, LORA), F32)

    tm_prep = 256
    tix = np.arange(tm_prep)
    same_chunk = (tix[:, None] // RW_CHUNK) == (tix[None, :] // RW_CHUNK)
    hix = np.arange(RWKV_WIDTH) // RWKV_HEAD_DIM
    prep = dict(
        mu_r=row(rwkv_mu_r[0]), mu_k=row(rwkv_mu_k[0]), mu_v=row(rwkv_mu_v[0]),
        mu_m=jnp.concatenate([z64, row(rwkv_mu_w[0]), row(rwkv_mu_a[0]), row(rwkv_mu_g[0])], axis=1),
        w0=row(rwkv_w0[0]), w2=_pad_rows(rwkv_w2[0], LORA),
        a0=row(rwkv_a0[0]), a2=_pad_rows(rwkv_a2[0], 0), g2=_pad_rows(rwkv_g2[0], LORA),
        k_k=row(rwkv_k_k[0]), k_a=row(rwkv_k_a[0]), r_k=row(rwkv_r_k[0]),
        bd=jnp.asarray(hix[:, None] == hix[None, :], BF16),
        tri=jnp.asarray(same_chunk & (tix[:, None] >= tix[None, :]), BF16),
        blk=jnp.asarray(same_chunk, BF16),
    )

    h1 = _ffn(h, row(ffn1_norm[0]), ffn1_w_gate[0].astype(BF16), ffn1_w_up[0].astype(BF16),
              ffn1_w_down[0].astype(BF16), row(final_norm), final_norm=False)
    u = _inproj(h1, row(mix_norm[0]), _relayout_w_in(w_in[0]))

    q, k, vt, qi, ki, wt = _attn_prep(u, _rope_tables(s, ATTN_HEAD_DIM), _rope_tables(s, IDX_HEAD_DIM),
                                     tm=DSA_KEY_TILE)
    attn = _dsa(q, qi, wt, k, vt, ki)

    at, rt, bt, kt, rv, bdt, kdt, pct, bonus, g = _rwkv_prep(u, prep, tm=tm_prep)
    y = _rwkv_state(*_rwkv_chunk(at, rt, bt, kt, rv, bdt, kdt, pct))

    wo = w_out[0].astype(BF16)
    h2 = _outproj(h1, attn, y, bonus, g, row(rwkv_gn_gain[0]), row(rwkv_gn_bias[0]), prep["bd"],
                  wo[:ATTN_WIDTH], wo[ATTN_WIDTH:])
    out = _ffn(h2, row(ffn2_norm[0]), ffn2_w_gate[0].astype(BF16), ffn2_w_up[0].astype(BF16),
               ffn2_w_down[0].astype(BF16), row(final_norm), final_norm=True)
    return out[None]
```

```python
import functools

import jax
import jax.numpy as jnp
import numpy as np
from jax import lax
from jax.experimental import pallas as pl
from jax.experimental.pallas import tpu as pltpu

F32 = jnp.float32
BF16 = jnp.bfloat16
I32 = jnp.int32

D_MODEL = 2048
CHUNK = 64
ROPE_THETA = 500000.0
ROPE_FRACTION = 4
NORM_EPS = 1e-6
ATTN_HEAD_DIM = 128
ATTN_WIDTH = 1024
ATTN_HEADS = 8
ATTN_KV_HEADS = 2
KV_WIDTH = 256
IDX_HEADS = 8
IDX_HEAD_DIM = 64
TOPK_MAX = 256
RWKV_HEAD_DIM = 64
RWKV_WIDTH = 1024
RWKV_HEADS = 16
LORA = 64
GN_EPS = 64e-5
D_FF = 5632

LANES = 128
VMEM_LIMIT = 56 * 1024 * 1024

U_WIDTH = 5632
U_Q = 0
U_KV = 1024
U_QI = 1536
U_RR = 2048
U_RK = 3072
U_RV = 4096
U_MISC = 5120

RW_CHUNK = 64
DSA_KEY_TILE = 512
VT_ROWS = ATTN_HEAD_DIM + 16
EXP2_HEADROOM = 60.0
DEN_MIN, DEN_MAX = 1e-30, 1e30
CNT_ROWS = 128
I16 = jnp.int16
I16_MIN = -2 ** 15
LOG2E = 1.4426950408889634
INT_MIN = -2 ** 31
NEG_BIG = -1e30


def _rms(x, g):
    return x * lax.rsqrt(jnp.mean(x * x, axis=-1, keepdims=True) + NORM_EPS) * g


def _dot(a, b):
    return jnp.dot(a, b, preferred_element_type=F32)


def _dot_nt(a, b):
    return lax.dot_general(a, b, (((1,), (1,)), ((), ())), preferred_element_type=F32)


def _split3(x):
    hi = x.astype(BF16)
    r1 = x - hi.astype(F32)
    mid = r1.astype(BF16)
    lo = (r1 - mid.astype(F32)).astype(BF16)
    return hi, mid, lo


def _dot_seg(x, m):
    hi, lo = _split2(x)
    return _dot(hi, m) + _dot(lo, m)


def _dot_exact_lhs(m, x):
    hi, mid, lo = _split3(x)
    return _dot(m, hi) + _dot(m, mid) + _dot(m, lo)


def _split2(x):
    hi = x.astype(BF16)
    lo = (x - hi.astype(F32)).astype(BF16)
    return hi, lo


def _mm3(a, b):
    ah, al = _split2(a)
    bh, bl = _split2(b)
    return _dot(ah, bh) + _dot(ah, bl) + _dot(al, bh)


def _mm3_nt(a, b):
    ah, al = _split2(a)
    bh, bl = _split2(b)
    return _dot_nt(ah, bh) + _dot_nt(ah, bl) + _dot_nt(al, bh)


def _ffn_body(x_ref, g_ref, wg_ref, wu_ref, wd_ref, fg_ref, o_ref, xn_ref, acc_ref, *, final_norm):
    j = pl.program_id(1)

    @pl.when(j == 0)
    def _():
        xn_ref[...] = _rms(x_ref[...], g_ref[...]).astype(BF16)
        acc_ref[...] = jnp.zeros_like(acc_ref)

    xn = xn_ref[...]
    gt = _dot(xn, wg_ref[...])
    ut = _dot(xn, wu_ref[...])
    act = (gt * jax.nn.sigmoid(gt) * ut).astype(BF16)
    acc_ref[...] += _dot(act, wd_ref[...])

    @pl.when(j == pl.num_programs(1) - 1)
    def _():
        h = x_ref[...] + 0.5 * acc_ref[...]
        if final_norm:
            h = _rms(h, fg_ref[...])
        o_ref[...] = h


def _ffn(x, gain, wg, wu, wd, fgain, *, final_norm, tm=512, tf=512):
    s, d = x.shape
    f = wg.shape[1]
    return pl.pallas_call(
        functools.partial(_ffn_body, final_norm=final_norm),
        name="ffn_final" if final_norm else "ffn",
        grid=(s // tm, f // tf),
        in_specs=[
            pl.BlockSpec((tm, d), lambda i, j: (i, 0)),
            pl.BlockSpec((1, d), lambda i, j: (0, 0)),
            pl.BlockSpec((d, tf), lambda i, j: (0, j)),
            pl.BlockSpec((d, tf), lambda i, j: (0, j)),
            pl.BlockSpec((tf, d), lambda i, j: (j, 0)),
            pl.BlockSpec((1, d), lambda i, j: (0, 0)),
        ],
        out_specs=pl.BlockSpec((tm, d), lambda i, j: (i, 0)),
        out_shape=jax.ShapeDtypeStruct((s, d), F32),
        scratch_shapes=[pltpu.VMEM((tm, d), BF16), pltpu.VMEM((tm, d), F32)],
        compiler_params=pltpu.CompilerParams(
            dimension_semantics=("parallel", "arbitrary"), vmem_limit_bytes=VMEM_LIMIT),
    )(x, gain, wg, wu, wd, fgain)


def _inproj_body(x_ref, g_ref, w_ref, o_ref, xn_ref):
    @pl.when(pl.program_id(1) == 0)
    def _():
        xn_ref[...] = _rms(x_ref[...], g_ref[...]).astype(BF16)

    o_ref[...] = _dot(xn_ref[...], w_ref[...])


def _inproj(h, gain, w, *, tm=1024, tn=512):
    s, d = h.shape
    n = w.shape[1]
    return pl.pallas_call(
        _inproj_body,
        name="inproj",
        grid=(s // tm, n // tn),
        in_specs=[
            pl.BlockSpec((tm, d), lambda i, j: (i, 0)),
            pl.BlockSpec((1, d), lambda i, j: (0, 0)),
            pl.BlockSpec((d, tn), lambda i, j: (0, j)),
        ],
        out_specs=pl.BlockSpec((tm, tn), lambda i, j: (i, j)),
        out_shape=jax.ShapeDtypeStruct((s, n), F32),
        scratch_shapes=[pltpu.VMEM((tm, d), BF16)],
        compiler_params=pltpu.CompilerParams(
            dimension_semantics=("parallel", "arbitrary"), vmem_limit_bytes=VMEM_LIMIT),
    )(h, gain, w)


def _rope_tables(s, head_dim):
    rot = head_dim // ROPE_FRACTION
    half = rot // 2
    inv_freq = ROPE_THETA ** (-(jnp.arange(half, dtype=F32) * 2.0 / rot))
    ang = jnp.arange(s, dtype=F32)[:, None] * inv_freq[None, :]
    cos, sin = jnp.cos(ang), jnp.sin(ang)
    ones = jnp.ones((s, head_dim - rot), F32)
    zeros = jnp.zeros((s, head_dim - rot), F32)
    zh = jnp.zeros((s, half), F32)
    c = jnp.concatenate([cos, cos, ones], axis=1)
    sp = jnp.concatenate([zh, sin, zeros], axis=1)
    sm = jnp.concatenate([-sin, zh, zeros], axis=1)
    reps = LANES // head_dim
    return tuple(jnp.tile(t, (1, reps)) for t in (c, sp, sm))


def _rope128(x, c, sp, sm, half):
    return x * c + pltpu.roll(x, half, 1) * sp + pltpu.roll(x, LANES - half, 1) * sm


def _attn_prep_body(q_ref, kv_ref, qi_ref, misc_ref, c1_ref, sp1_ref, sm1_ref, c2_ref, sp2_ref, sm2_ref,
                    qo_ref, ko_ref, vo_ref, qio_ref, kio_ref, wio_ref):
    c1, sp1, sm1 = c1_ref[...], sp1_ref[...], sm1_ref[...]
    c2, sp2, sm2 = c2_ref[...], sp2_ref[...], sm2_ref[...]
    att_scale = ATTN_HEAD_DIM ** -0.5 * LOG2E
    for b in range(ATTN_WIDTH // LANES):
        sl = slice(b * LANES, (b + 1) * LANES)
        qo_ref[:, sl] = (_rope128(q_ref[:, sl], c1, sp1, sm1, 16) * att_scale).astype(BF16)
    for b in range(KV_WIDTH // LANES):
        sl = slice(b * LANES, (b + 1) * LANES)
        ko_ref[:, sl] = _rope128(kv_ref[:, sl], c1, sp1, sm1, 16).astype(BF16)
    vt = kv_ref[:, KV_WIDTH:].T.astype(BF16)
    ones = jnp.ones((VT_ROWS - ATTN_HEAD_DIM, vt.shape[1]), BF16)
    for g in range(ATTN_KV_HEADS):
        vo_ref[0, g * VT_ROWS:g * VT_ROWS + ATTN_HEAD_DIM, :] = vt[g * ATTN_HEAD_DIM:(g + 1) * ATTN_HEAD_DIM]
        vo_ref[0, g * VT_ROWS + ATTN_HEAD_DIM:(g + 1) * VT_ROWS, :] = ones
    for b in range(IDX_HEADS * IDX_HEAD_DIM // LANES):
        sl = slice(b * LANES, (b + 1) * LANES)
        qio_ref[:, sl] = _rope128(qi_ref[:, sl], c2, sp2, sm2, 8).astype(BF16)
    ki = _rope128(misc_ref[:, :LANES], c2, sp2, sm2, 8)
    kio_ref[...] = ki[:, :IDX_HEAD_DIM].astype(BF16)
    idx_scale = (IDX_HEADS ** -0.5) * (IDX_HEAD_DIM ** -0.5)
    wio_ref[...] = (misc_ref[:, 2 * LANES:3 * LANES] * idx_scale).T[:IDX_HEADS]


def _attn_prep(u, tabs128, tabs64, *, tm):
    s = u.shape[0]
    tab_spec = pl.BlockSpec((tm, LANES), lambda i: (i, 0))
    return pl.pallas_call(
        _attn_prep_body,
        name="attn_prep",
        grid=(s // tm,),
        in_specs=[
            pl.BlockSpec((tm, ATTN_WIDTH), lambda i: (i, U_Q // ATTN_WIDTH)),
            pl.BlockSpec((tm, 512), lambda i: (i, U_KV // 512)),
            pl.BlockSpec((tm, 512), lambda i: (i, U_QI // 512)),
            pl.BlockSpec((tm, 512), lambda i: (i, U_MISC // 512)),
        ] + [tab_spec] * 6,
        out_specs=[
            pl.BlockSpec((tm, ATTN_WIDTH), lambda i: (i, 0)),
            pl.BlockSpec((tm, KV_WIDTH), lambda i: (i, 0)),
            pl.BlockSpec((1, ATTN_KV_HEADS * VT_ROWS, tm), lambda i: (i, 0, 0)),
            pl.BlockSpec((tm, 512), lambda i: (i, 0)),
            pl.BlockSpec((tm, IDX_HEAD_DIM), lambda i: (i, 0)),
            pl.BlockSpec((IDX_HEADS, tm), lambda i: (0, i)),
        ],
        out_shape=[
            jax.ShapeDtypeStruct((s, ATTN_WIDTH), BF16),
            jax.ShapeDtypeStruct((s, KV_WIDTH), BF16),
            jax.ShapeDtypeStruct((s // tm, ATTN_KV_HEADS * VT_ROWS, tm), BF16),
            jax.ShapeDtypeStruct((s, 512), BF16),
            jax.ShapeDtypeStruct((s, IDX_HEAD_DIM), BF16),
            jax.ShapeDtypeStruct((IDX_HEADS, s), F32),
        ],
        compiler_params=pltpu.CompilerParams(
            dimension_semantics=("parallel",), vmem_limit_bytes=VMEM_LIMIT),
    )(u, u, u, u, *tabs128, *tabs64)


def _col_reduce(x, op):
    n, w = x.shape
    y = x.reshape(n // 64, 64, w)
    acc = y[0]
    for a in range(1, n // 64):
        acc = op(acc, y[a])
    z = acc.reshape(8, 8, w)
    t = [op(z[2 * i], z[2 * i + 1]) for i in range(4)]
    r = op(op(t[0], t[1]), op(t[2], t[3]))
    red = jnp.max if op is jnp.maximum else jnp.sum
    return red(r, axis=0, keepdims=True)


def _dsa_body(q_ref, qi_ref, wt_ref, k_ref, vt_ref, ki_ref, o_ref,
              key_ref, hi_ref, lo_ref, qs_ref, qis_ref, p1_ref, p2_ref, r1_ref, r2_ref, run_ref, accr_ref,
              flag_ref, s_ref, s2_ref, smax_ref, smax2_ref, m_ref, acc_ref, *, tq, tk, k_sel, pos_bits):
    qb = pl.program_id(0)
    q_end = (qb + 1) * tq
    n_kt = (q_end + tk - 1) // tk
    n_full = (qb * tq) // tk
    n_pair = ATTN_HEADS // 2

    for h in range(ATTN_HEADS):
        qs_ref[h * tq:(h + 1) * tq, :] = q_ref[:, h * ATTN_HEAD_DIM:(h + 1) * ATTN_HEAD_DIM]
    for h in range(IDX_HEADS):
        qis_ref[h * tq:(h + 1) * tq, :] = qi_ref[:, h * IDX_HEAD_DIM:(h + 1) * IDX_HEAD_DIM]

    qpos = lax.broadcasted_iota(I32, (1, tq), 1) + qb * tq
    key_lim = (qpos // CHUNK + 1) * CHUNK
    wt = wt_ref[...]

    def score_tile(jt, masked):
        kit = ki_ref[pl.ds(pl.multiple_of(jt * tk, tk), tk), :]
        acc = jnp.zeros((tk, tq), F32)
        for p in range(IDX_HEADS // 2):
            s2 = _dot_nt(kit, qis_ref[2 * p * tq:(2 * p + 2) * tq, :])
            acc = acc + jnp.maximum(s2[:, :tq], 0.0) * wt[2 * p:2 * p + 1, :]
            acc = acc + jnp.maximum(s2[:, tq:], 0.0) * wt[2 * p + 1:2 * p + 2, :]
        bits = pltpu.bitcast(acc, I32)
        key = bits ^ ((bits >> 31) & 0x7FFFFFFF)
        if masked:
            kpos = lax.broadcasted_iota(I32, (tk, tq), 0) + jt * tk
            key = jnp.where(kpos < key_lim, key, INT_MIN)
        key_ref[jt] = key
        hi_ref[jt] = (key >> 16).astype(I16)
        lo_ref[jt] = (key ^ 0x8000).astype(I16)

    def two_full_tiles(i, c):
        score_tile(2 * i, False)
        score_tile(2 * i + 1, False)
        return c

    def diag_tile(jt, c):
        score_tile(jt, True)
        return c

    lax.fori_loop(0, n_full // 2, two_full_tiles, 0)
    lax.fori_loop(n_full - n_full % 2, n_kt, diag_tile, 0)

    @pl.when(n_kt % 2 == 1)
    def _():
        key_ref[n_kt] = jnp.full((tk, tq), INT_MIN, I32)

    one16 = jnp.ones((tk, tq), I16)
    zero16 = jnp.zeros((tk, tq), I16)

    def fold(cnt, hit):
        hit = hit.reshape(tk // CNT_ROWS, CNT_ROWS, tq)
        for a in range(tk // CNT_ROWS):
            cnt = cnt + hit[a]
        return cnt

    def total(cnt):
        return jnp.sum(cnt.astype(I32), axis=0, keepdims=True)

    def search16(half_ref, need):
        def bit_step(i, base):
            cand = base + lax.shift_left(jnp.int32(1), 15 - i)
            cand16 = cand.astype(I16)

            def count_tile(jt, cnt):
                return fold(cnt, jnp.where(half_ref[jt] >= cand16, one16, zero16))

            cnt = lax.fori_loop(0, n_kt, count_tile, jnp.zeros((CNT_ROWS, tq), I16))
            return jnp.where(total(cnt) >= need, cand, base)

        return lax.fori_loop(0, 16, bit_step, jnp.full((1, tq), I16_MIN, I32))

    hi_thr = search16(hi_ref, k_sel)
    hi_thr16 = hi_thr.astype(I16)

    def bucket_tile(jt, cnt):
        hi = hi_ref[jt]
        lo_ref[jt] = jnp.where(hi == hi_thr16, lo_ref[jt], jnp.int16(I16_MIN))
        return fold(cnt, jnp.where(hi > hi_thr16, one16, zero16))

    above = total(lax.fori_loop(0, n_kt, bucket_tile, jnp.zeros((CNT_ROWS, tq), I16)))
    lo_thr = search16(lo_ref, k_sel - above)
    thr_raw = hi_thr * 65536 + (lo_thr + 32768)
    thr = jnp.maximum(thr_raw, INT_MIN + 1)

    lo_thr16 = lo_thr.astype(I16)

    def tie_count_tile(jt, carry):
        lo = lo_ref[jt]
        return (fold(carry[0], jnp.where(lo >= lo_thr16, one16, zero16)),
                fold(carry[1], jnp.where(lo > lo_thr16, one16, zero16)))

    zc = jnp.zeros((CNT_ROWS, tq), I16)
    c_ge, c_gt = lax.fori_loop(0, n_kt, tie_count_tile, (zc, zc))
    has_tie = ((above + total(c_ge)) > k_sel) & (thr_raw > INT_MIN)
    keep = jnp.where(has_tie, k_sel - (above + total(c_gt)), jnp.int32(2 ** 30))

    @pl.when(jnp.max(jnp.where(has_tie, 1, 0)) > 0)
    def _():
        rows = lax.broadcasted_iota(I32, (tk, tq), 0)

        def tied_before(limit):
            def tile(jt, cnt):
                hit = (key_ref[jt] == thr_raw) & (rows + jt * tk < limit)
                return cnt + jnp.sum(jnp.where(hit, 1, 0).reshape(tk // 8, 8, tq), axis=0)
            return jnp.sum(lax.fori_loop(0, n_kt, tile, jnp.zeros((8, tq), I32)), axis=0, keepdims=True)

        def pos_step(i, base):
            cand = base + lax.shift_left(jnp.int32(1), pos_bits - 1 - i)
            return jnp.where(tied_before(cand) < keep, cand, base)

        last = lax.fori_loop(0, pos_bits, pos_step, jnp.zeros((1, tq), I32))

        def retire_tile(jt, c):
            key = key_ref[jt]
            drop = (key == thr_raw) & (rows + jt * tk > last) & has_tie
            key_ref[jt] = jnp.where(drop, INT_MIN, key)
            return c

        lax.fori_loop(0, n_kt, retire_tile, 0)

    def masked_scores(jt, p, sel2):
        g = p // (n_pair // ATTN_KV_HEADS)
        off = pl.multiple_of(jt * tk, tk)
        kg = k_ref[pl.ds(off, tk), g * ATTN_HEAD_DIM:(g + 1) * ATTN_HEAD_DIM]
        s = _dot_nt(kg, qs_ref[2 * p * tq:(2 * p + 2) * tq, :])
        return s if sel2 is None else jnp.where(sel2, s, NEG_BIG)

    def selection(jt):
        sel = key_ref[jt] >= thr
        return jnp.concatenate([sel, sel], axis=1)

    def values(jt, p):
        g = p // (n_pair // ATTN_KV_HEADS)
        return vt_ref[jt, g * VT_ROWS:(g + 1) * VT_ROWS, :]

    def two_tiles_loop(stage, fold_in, bufs):
        stage(bufs[0], 0)

        def trip(i, c):
            j0 = 2 * i
            stage(bufs[1], j0 + 1)
            fold_in(bufs[0], j0)
            stage(bufs[0], jnp.minimum(j0 + 2, n_kt - 1))
            fold_in(bufs[1], j0 + 1)
            return c

        lax.fori_loop(0, (n_kt + 1) // 2, trip, 0)

    for p in range(n_pair):
        r0 = _col_reduce(masked_scores(0, p, None), jnp.maximum)
        run_ref[p] = r0
        accr_ref[p] = r0
    flag_ref[...] = jnp.full(flag_ref.shape, NEG_BIG, F32)
    acc_ref[...] = jnp.zeros(acc_ref.shape, F32)

    def stage_probs(buf, jt):
        p_ref, r_ref = buf
        sel2 = selection(jt)
        for p in range(n_pair):
            s = masked_scores(jt, p, sel2)
            r = run_ref[p]
            part = None
            for c0 in range(0, tk, 64):
                sc = s[c0:c0 + 64]
                p_ref[p, c0:c0 + 64, :] = jnp.exp2(sc - r).astype(BF16)
                part = sc if part is None else jnp.maximum(part, sc)
            tmax = _col_reduce(part, jnp.maximum)
            r_ref[p] = r
            flag_ref[p] = jnp.maximum(flag_ref[p], tmax - r)
            run_ref[p] = jnp.maximum(r, tmax)

    def fold_probs(buf, jt):
        p_ref, r_ref = buf
        for p in range(n_pair):
            r = r_ref[p]
            acc_ref[p] = acc_ref[p] * jnp.exp2(accr_ref[p] - r) + _dot(values(jt, p), p_ref[p])
            accr_ref[p] = r

    two_tiles_loop(stage_probs, fold_probs, ((p1_ref, r1_ref), (p2_ref, r2_ref)))

    bad = jnp.float32(0.0)
    for p in range(n_pair):
        den = acc_ref[p][ATTN_HEAD_DIM:ATTN_HEAD_DIM + 1]
        ok = (flag_ref[p] < EXP2_HEADROOM) & (den > DEN_MIN) & (den < DEN_MAX)
        bad = jnp.maximum(bad, jnp.max(jnp.where(ok, 0.0, 1.0)))

    @pl.when(bad > 0.0)
    def _():
        m_ref[...] = jnp.full(m_ref.shape, NEG_BIG, F32)
        acc_ref[...] = jnp.zeros(acc_ref.shape, F32)

        def stage_scores(buf, jt):
            s_buf, max_buf = buf
            sel2 = selection(jt)
            for p in range(n_pair):
                s = masked_scores(jt, p, sel2)
                s_buf[p] = s
                max_buf[p] = _col_reduce(s, jnp.maximum)

        def fold_scores(buf, jt):
            s_buf, max_buf = buf
            for p in range(n_pair):
                m_prev = m_ref[p]
                m_cur = jnp.maximum(m_prev, max_buf[p])
                pr = jnp.exp2(s_buf[p] - m_cur).astype(BF16)
                acc_ref[p] = acc_ref[p] * jnp.exp2(m_prev - m_cur) + _dot(values(jt, p), pr)
                m_ref[p] = m_cur

        two_tiles_loop(stage_scores, fold_scores, ((s_ref, smax_ref), (s2_ref, smax2_ref)))

    for p in range(n_pair):
        acc = acc_ref[p]
        o = acc[:ATTN_HEAD_DIM] / acc[ATTN_HEAD_DIM:ATTN_HEAD_DIM + 1]
        for r in range(2):
            h = 2 * p + r
            o_ref[:, h * ATTN_HEAD_DIM:(h + 1) * ATTN_HEAD_DIM] = o[:, r * tq:(r + 1) * tq].T.astype(BF16)


def _dsa(q, qi, wt, k, vt, ki, *, tq=128):
    s = q.shape[0]
    tk = vt.shape[2]
    k_sel = min(TOPK_MAX, s // 4)
    whole = lambda shape: pl.BlockSpec(shape, lambda i: (0,) * len(shape), pipeline_mode=pl.Buffered(1))
    return pl.pallas_call(
        functools.partial(_dsa_body, tq=tq, tk=tk, k_sel=k_sel, pos_bits=(s - 1).bit_length()),
        name="dsa",
        grid=(s // tq,),
        in_specs=[
            pl.BlockSpec((tq, ATTN_WIDTH), lambda i: (i, 0)),
            pl.BlockSpec((tq, 512), lambda i: (i, 0)),
            pl.BlockSpec((IDX_HEADS, tq), lambda i: (0, i)),
            whole((s, KV_WIDTH)),
            whole((s // tk, ATTN_KV_HEADS * VT_ROWS, tk)),
            whole((s, IDX_HEAD_DIM)),
        ],
        out_specs=pl.BlockSpec((tq, ATTN_WIDTH), lambda i: (i, 0)),
        out_shape=jax.ShapeDtypeStruct((s, ATTN_WIDTH), BF16),
        scratch_shapes=[
            pltpu.VMEM((s // tk, tk, tq), I32),
            pltpu.VMEM((s // tk, tk, tq), I16),
            pltpu.VMEM((s // tk, tk, tq), I16),
            pltpu.VMEM((ATTN_HEADS * tq, ATTN_HEAD_DIM), BF16),
            pltpu.VMEM((IDX_HEADS * tq, IDX_HEAD_DIM), BF16),
            pltpu.VMEM((ATTN_HEADS // 2, tk, 2 * tq), BF16),
            pltpu.VMEM((ATTN_HEADS // 2, tk, 2 * tq), BF16),
            pltpu.VMEM((ATTN_HEADS // 2, 1, 2 * tq), F32),
            pltpu.VMEM((ATTN_HEADS // 2, 1, 2 * tq), F32),
            pltpu.VMEM((ATTN_HEADS // 2, 1, 2 * tq), F32),
            pltpu.VMEM((ATTN_HEADS // 2, 1, 2 * tq), F32),
            pltpu.VMEM((ATTN_HEADS // 2, 1, 2 * tq), F32),
            pltpu.VMEM((ATTN_HEADS // 2, tk, 2 * tq), F32),
            pltpu.VMEM((ATTN_HEADS // 2, tk, 2 * tq), F32),
            pltpu.VMEM((ATTN_HEADS // 2, 1, 2 * tq), F32),
            pltpu.VMEM((ATTN_HEADS // 2, 1, 2 * tq), F32),
            pltpu.VMEM((ATTN_HEADS // 2, 1, 2 * tq), F32),
            pltpu.VMEM((ATTN_HEADS // 2, VT_ROWS, 2 * tq), F32),
        ],
        compiler_params=pltpu.CompilerParams(
            dimension_semantics=("parallel",), vmem_limit_bytes=VMEM_LIMIT),
    )(q, qi, wt, k, vt, ki)


def _token_shift(x, prev8, first):
    rolled = pltpu.roll(x, 1, 0)
    prev_row = jnp.where(first, 0.0, prev8[7:8, :])
    row = lax.broadcasted_iota(I32, x.shape, 0)
    return jnp.where(row == 0, jnp.broadcast_to(prev_row, x.shape), rolled)


def _rwkv_prep_body(rr_ref, rk_ref, rv_ref, ms_ref, prr_ref, prk_ref, prv_ref, pms_ref,
                    mur_ref, muk_ref, muv_ref, mum_ref, w0_ref, w2_ref, a0_ref, a2_ref, g2_ref,
                    kk_ref, ka_ref, rkk_ref, bd_ref, tri_ref, blk_ref,
                    at_ref, rt_ref, bt_ref, kt_ref, v_ref, bdt_ref, kdt_ref, pct_ref, bonus_ref, g_ref):
    first = pl.program_id(0) == 0

    def lerp(x_ref, p_ref, mu_ref):
        x = x_ref[...]
        return x + (_token_shift(x, p_ref[...], first) - x) * mu_ref[...]

    r = lerp(rr_ref, prr_ref, mur_ref)
    k = lerp(rk_ref, prk_ref, muk_ref)
    v = lerp(rv_ref, prv_ref, muv_ref)
    ms = lerp(ms_ref, pms_ref, mum_ref)
    xw = ms[:, :LANES]
    xag = ms[:, LANES:]

    w = w0_ref[...] + _dot(jnp.tanh(xw).astype(BF16), w2_ref[...])
    z = -w
    softplus = jnp.maximum(z, 0.0) + jnp.log1p(jnp.exp(-jnp.abs(z)))
    lw = -jnp.exp(-softplus - 0.5)
    a_lr = jax.nn.sigmoid(a0_ref[...] + _dot(xag.astype(BF16), a2_ref[...]))
    g_ref[...] = _dot(jax.nn.sigmoid(xag).astype(BF16), g2_ref[...])

    bd = bd_ref[...]
    kk = k * kk_ref[...]
    nrm = jnp.sqrt(_dot_seg(kk * kk, bd))
    kk = kk / jnp.maximum(nrm, 1e-12)
    k = k * (1.0 + (a_lr - 1.0) * ka_ref[...])
    bonus_ref[...] = _dot_seg(r * k * rkk_ref[...], bd) * v

    cum = _dot_exact_lhs(tri_ref[...], lw)
    tot = _dot_exact_lhs(blk_ref[...], lw)
    e_neg = jnp.exp(-cum)
    e_dec = jnp.exp(tot - cum)
    b = kk * a_lr
    at_ref[...] = -kk * jnp.exp(cum - lw)
    rt_ref[...] = r * jnp.exp(cum)
    bt_ref[...] = b * e_neg
    kt_ref[...] = k * e_neg
    v_ref[...] = v
    bdt_ref[...] = (b * e_dec).T
    kdt_ref[...] = (k * e_dec).T
    pct_ref[...] = jnp.exp(tot).T


def _rwkv_prep(u, p, *, tm=512):
    s = u.shape[0]
    w = RWKV_WIDTH
    t8 = tm // 8
    tile = lambda col: pl.BlockSpec((tm, w), lambda i: (i, col // w))
    prev = lambda col: pl.BlockSpec((8, w), lambda i: (jnp.maximum(i * t8 - 1, 0), col // w))
    vec = lambda n: pl.BlockSpec((1, n), lambda i: (0, 0))
    mat = lambda a, b: pl.BlockSpec((a, b), lambda i: (0, 0))
    tm_spec = pl.BlockSpec((tm, w), lambda i: (i, 0))
    cm_spec = pl.BlockSpec((w, tm), lambda i: (0, i))
    outs = pl.pallas_call(
        _rwkv_prep_body,
        name="rwkv_prep",
        grid=(s // tm,),
        in_specs=[
            tile(U_RR), tile(U_RK), tile(U_RV),
            pl.BlockSpec((tm, 256), lambda i: (i, U_MISC // 256)),
            prev(U_RR), prev(U_RK), prev(U_RV),
            pl.BlockSpec((8, 256), lambda i: (jnp.maximum(i * t8 - 1, 0), U_MISC // 256)),
            vec(w), vec(w), vec(w), vec(256),
            vec(w), mat(LANES, w), vec(w), mat(LANES, w), mat(LANES, w),
            vec(w), vec(w), vec(w),
            mat(w, w), mat(tm, tm), mat(tm, tm),
        ],
        out_specs=[tm_spec] * 5 + [cm_spec] * 3 + [tm_spec] * 2,
        out_shape=[jax.ShapeDtypeStruct((s, w), F32)] * 5
        + [jax.ShapeDtypeStruct((w, s), F32)] * 3
        + [jax.ShapeDtypeStruct((s, w), F32)] * 2,
        compiler_params=pltpu.CompilerParams(
            dimension_semantics=("parallel",), vmem_limit_bytes=VMEM_LIMIT),
    )(u, u, u, u, u, u, u, u,
      p["mu_r"], p["mu_k"], p["mu_v"], p["mu_m"], p["w0"], p["w2"], p["a0"], p["a2"], p["g2"],
      p["k_k"], p["k_a"], p["r_k"], p["bd"], p["tri"], p["blk"])
    return outs


def _mm1(a, b):
    return _dot(a.astype(BF16), b.astype(BF16))


def _rwkv_chunk_body(at_ref, rt_ref, bt_ref, kt_ref, v_ref, bdt_ref, kdt_ref, pct_ref,
                     mc_ref, nc_ref, yc_ref, y0_ref, *, heads, chunks):
    n = RWKV_HEAD_DIM
    cl = RW_CHUNK
    ri = lax.broadcasted_iota(I32, (cl, cl), 0)
    ci = lax.broadcasted_iota(I32, (cl, cl), 1)
    strict = ri > ci
    incl = ri >= ci
    same16 = (ri // 16) == (ci // 16)
    eye = (ri == ci).astype(F32)

    pairs = [(slice(cc * cl, (cc + 1) * cl), slice(hh * n, (hh + 1) * n))
             for hh in range(heads) for cc in range(chunks)]
    each = lambda fn, *lists: [fn(*xs) for xs in zip(*lists)]

    at = [at_ref[ts, ls] for ts, ls in pairs]
    rt = [rt_ref[ts, ls] for ts, ls in pairs]
    v = [v_ref[ts, ls] for ts, ls in pairs]
    g = [_mm3_nt(jnp.concatenate([a, r], axis=0), jnp.concatenate([bt_ref[ts, ls], kt_ref[ts, ls]], axis=0))
         for a, r, (ts, ls) in zip(at, rt, pairs)]
    l = [jnp.where(strict, x[:cl, :cl], 0.0) for x in g]
    a_ak = [jnp.where(strict, x[:cl, cl:], 0.0) for x in g]
    a_rb = [jnp.where(incl, x[cl:, :cl], 0.0) for x in g]
    a_rk = [jnp.where(incl, x[cl:, cl:], 0.0) for x in g]

    ld = [jnp.where(same16, x, 0.0) for x in l]
    lo = each(lambda x, y: x - y, l, ld)
    d = [eye + x for x in ld]
    pw = ld
    for _ in range(3):
        pw = each(_mm1, pw, pw)
        d = each(lambda x, y: x + _mm1(x, y), d, pw)
    m = each(_mm1, d, lo)
    m2 = each(_mm1, m, m)
    f = each(lambda x, y: x + _mm1(y, x), d, m2)
    t = each(lambda x, y: x + _mm1(y, x), f, m)

    w1 = each(_mm1, a_ak, v)
    x = each(lambda tt, a, w: _mm1(tt, jnp.concatenate([a, w], axis=1)), t, at, w1)
    big = [_mm1(jnp.concatenate([bdt_ref[ls, ts], arb], axis=0), xx)
           for arb, xx, (ts, ls) in zip(a_rb, x, pairs)]
    kv = [_mm1(jnp.concatenate([kdt_ref[ls, ts], ark], axis=0), vv)
          for ark, vv, (ts, ls) in zip(a_rk, v, pairs)]
    for i, (ts, ls) in enumerate(pairs):
        mc_ref[ts, ls] = eye * pct_ref[ls, ts] + big[i][:n, :n]
        nc_ref[ts, ls] = big[i][:n, n:] + kv[i][:n]
        yc_ref[ts, ls] = rt[i] + big[i][n:, :n]
        y0_ref[ts, ls] = big[i][n:, n:] + kv[i][n:]


def _rwkv_chunk(at, rt, bt, kt, v, bdt, kdt, pct, *, rows=512, lanes=256):
    s, w = at.shape
    tmaj = pl.BlockSpec((rows, lanes), lambda c, h: (c, h))
    cmaj = pl.BlockSpec((lanes, rows), lambda c, h: (h, c))
    out = jax.ShapeDtypeStruct((s, w), F32)
    return pl.pallas_call(
        functools.partial(_rwkv_chunk_body, heads=lanes // RWKV_HEAD_DIM, chunks=rows // RW_CHUNK),
        name="rwkv_chunk",
        grid=(s // rows, w // lanes),
        in_specs=[tmaj] * 5 + [cmaj] * 3,
        out_specs=[tmaj] * 4,
        out_shape=[out] * 4,
        compiler_params=pltpu.CompilerParams(
            dimension_semantics=("parallel", "parallel"), vmem_limit_bytes=VMEM_LIMIT),
    )(at, rt, bt, kt, v, bdt, kdt, pct)


def _rwkv_state_body(mc_ref, nc_ref, yc_ref, y0_ref, y_ref, st_ref, *, chunks):
    n = RWKV_HEAD_DIM
    cl = RW_CHUNK

    @pl.when(pl.program_id(0) == 0)
    def _():
        st_ref[...] = jnp.zeros_like(st_ref)

    lanes = [slice(hh * n, (hh + 1) * n) for hh in range(RWKV_HEADS)]
    st = [st_ref[:, ls] for ls in lanes]
    for cc in range(chunks):
        ts = slice(cc * cl, (cc + 1) * cl)
        r = [_mm3(jnp.concatenate([mc_ref[ts, ls], yc_ref[ts, ls]], axis=0), s)
             for ls, s in zip(lanes, st)]
        for ls, x in zip(lanes, r):
            y_ref[ts, ls] = x[n:] + y0_ref[ts, ls]
        st = [x[:n] + nc_ref[ts, ls] for ls, x in zip(lanes, r)]
    for ls, s in zip(lanes, st):
        st_ref[:, ls] = s


def _rwkv_state(mc, nc, yc, y0, *, chunks=4):
    s, w = mc.shape
    blk = pl.BlockSpec((chunks * RW_CHUNK, w), lambda c: (c, 0))
    return pl.pallas_call(
        functools.partial(_rwkv_state_body, chunks=chunks),
        name="rwkv_state",
        grid=(s // (chunks * RW_CHUNK),),
        in_specs=[blk] * 4,
        out_specs=blk,
        out_shape=jax.ShapeDtypeStruct((s, w), F32),
        scratch_shapes=[pltpu.VMEM((RWKV_HEAD_DIM, w), F32)],
        compiler_params=pltpu.CompilerParams(
            dimension_semantics=("arbitrary",), vmem_limit_bytes=VMEM_LIMIT),
    )(mc, nc, yc, y0)


def _outproj_body(h_ref, attn_ref, y_ref, bonus_ref, g_ref, gain_ref, bias_ref, bd_ref, wa_ref, wr_ref, o_ref):
    y = y_ref[...]
    bd = bd_ref[...]
    inv_n = 1.0 / RWKV_HEAD_DIM
    mean = _dot_seg(y, bd) * inv_n
    yc = y - mean
    var = _dot_seg(yc * yc, bd) * inv_n
    yn = yc * lax.rsqrt(var + GN_EPS) * gain_ref[...] + bias_ref[...]
    rw = ((yn + bonus_ref[...]) * g_ref[...]).astype(BF16)
    o_ref[...] = h_ref[...] + _dot(attn_ref[...], wa_ref[...]) + _dot(rw, wr_ref[...])


def _outproj(h, attn, y, bonus, g, gain, bias, bd, wa, wr, *, tm=256):
    s, d = h.shape
    w = RWKV_WIDTH
    row = lambda n: pl.BlockSpec((tm, n), lambda i: (i, 0))
    whole = lambda a, b: pl.BlockSpec((a, b), lambda i: (0, 0))
    return pl.pallas_call(
        _outproj_body,
        name="outproj",
        grid=(s // tm,),
        in_specs=[row(d), row(w), row(w), row(w), row(w), whole(1, w), whole(1, w), whole(w, w),
                  whole(ATTN_WIDTH, d), whole(w, d)],
        out_specs=row(d),
        out_shape=jax.ShapeDtypeStruct((s, d), F32),
        compiler_params=pltpu.CompilerParams(
            dimension_semantics=("parallel",), vmem_limit_bytes=VMEM_LIMIT),
    )(h, attn, y, bonus, g, gain, bias, bd, wa, wr)


def _relayout_w_in(w_in):
    d = w_in.shape[0]
    o = np.cumsum([0, 1024, 256, 256, 512, 64, 8, 1024, 1024, 1024, 64, 64, 64])
    seg = lambda i: w_in[:, o[i]:o[i + 1]]
    q, k, v, qi, ki, wi, rr, rk, rv, wd, ad, gd = (seg(i) for i in range(12))
    z = lambda n: jnp.zeros((d, n), w_in.dtype)
    cols = [q, k, v, qi, rr, rk, rv, ki, wd, ad, gd, wi, z(LANES - 8), z(LANES)]
    return jnp.concatenate(cols, axis=1).astype(BF16)


def _pad_rows(w2, lo):
    z = jnp.zeros_like(w2)
    return (jnp.concatenate([w2, z], axis=0) if lo == 0 else jnp.concatenate([z, w2], axis=0)).astype(BF16)


def kernel(x, ffn1_norm, ffn1_w_gate, ffn1_w_up, ffn1_w_down, mix_norm, w_in, w_out, rwkv_mu_r, rwkv_mu_k, rwkv_mu_v, rwkv_mu_w, rwkv_mu_a, rwkv_mu_g, rwkv_w0, rwkv_w2, rwkv_a0, rwkv_a2, rwkv_g2, rwkv_k_k, rwkv_k_a, rwkv_r_k, rwkv_gn_gain, rwkv_gn_bias, ffn2_norm, ffn2_w_gate, ffn2_w_up, ffn2_w_down, final_norm):
    b, s, d = x.shape
    assert b == 1 and d == D_MODEL and ffn1_norm.shape[0] == 1
    h = x[0]
    row = lambda a: a.reshape(1, -1).astype(F32)
    z64 = jnp.zeros((1, LORA), F32)

    tm_prep = 256
    tix = np.arange(tm_prep)
    same_chunk = (tix[:, None] // RW_CHUNK) == (tix[None, :] // RW_CHUNK)
    hix = np.arange(RWKV_WIDTH) // RWKV_HEAD_DIM
    prep = dict(
        mu_r=row(rwkv_mu_r[0]), mu_k=row(rwkv_mu_k[0]), mu_v=row(rwkv_mu_v[0]),
        mu_m=jnp.concatenate([z64, row(rwkv_mu_w[0]), row(rwkv_mu_a[0]), row(rwkv_mu_g[0])], axis=1),
        w0=row(rwkv_w0[0]), w2=_pad_rows(rwkv_w2[0], LORA),
        a0=row(rwkv_a0[0]), a2=_pad_rows(rwkv_a2[0], 0), g2=_pad_rows(rwkv_g2[0], LORA),
        k_k=row(rwkv_k_k[0]), k_a=row(rwkv_k_a[0]), r_k=row(rwkv_r_k[0]),
        bd=jnp.asarray(hix[:, None] == hix[None, :], BF16),
        tri=jnp.asarray(same_chunk & (tix[:, None] >= tix[None, :]), BF16),
        blk=jnp.asarray(same_chunk, BF16),
    )

    h1 = _ffn(h, row(ffn1_norm[0]), ffn1_w_gate[0].astype(BF16), ffn1_w_up[0].astype(BF16),
              ffn1_w_down[0].astype(BF16), row(final_norm), final_norm=False)
    u = _inproj(h1, row(mix_norm[0]), _relayout_w_in(w_in[0]))

    q, k, vt, qi, ki, wt = _attn_prep(u, _rope_tables(s, ATTN_HEAD_DIM), _rope_tables(s, IDX_HEAD_DIM),
                                     tm=DSA_KEY_TILE)
    attn = _dsa(q, qi, wt, k, vt, ki)

    at, rt, bt, kt, rv, bdt, kdt, pct, bonus, g = _rwkv_prep(u, prep, tm=tm_prep)
    y = _rwkv_state(*_rwkv_chunk(at, rt, bt, kt, rv, bdt, kdt, pct))

    wo = w_out[0].astype(BF16)
    h2 = _outproj(h1, attn, y, bonus, g, row(rwkv_gn_gain[0]), row(rwkv_gn_bias[0]), prep["bd"],
                  wo[:ATTN_WIDTH], wo[ATTN_WIDTH:])
    out = _ffn(h2, row(ffn2_norm[0]), ffn2_w_gate[0].astype(BF16), ffn2_w_up[0].astype(BF16),
               ffn2_w_down[0].astype(BF16), row(final_norm), final_norm=True)
    return out[None]
```

```python
import functools

import jax
import jax.numpy as jnp
import numpy as np
from jax import lax
from jax.experimental import pallas as pl
from jax.experimental.pallas import tpu as pltpu

F32 = jnp.float32
BF16 = jnp.bfloat16
I32 = jnp.int32

D_MODEL = 2048
CHUNK = 64
ROPE_THETA = 500000.0
ROPE_FRACTION = 4
NORM_EPS = 1e-6
ATTN_HEAD_DIM = 128
ATTN_WIDTH = 1024
ATTN_HEADS = 8
ATTN_KV_HEADS = 2
KV_WIDTH = 256
IDX_HEADS = 8
IDX_HEAD_DIM = 64
TOPK_MAX = 256
RWKV_HEAD_DIM = 64
RWKV_WIDTH = 1024
RWKV_HEADS = 16
LORA = 64
GN_EPS = 64e-5
D_FF = 5632

LANES = 128
VMEM_LIMIT = 56 * 1024 * 1024

U_WIDTH = 5632
U_Q = 0
U_KV = 1024
U_QI = 1536
U_RR = 2048
U_RK = 3072
U_RV = 4096
U_MISC = 5120

RW_CHUNK = 64
RW_GROUP_LANES = 512
DSA_KEY_TILE = 512
VT_ROWS = ATTN_HEAD_DIM + 16
EXP2_HEADROOM = 60.0
DEN_MIN, DEN_MAX = 1e-30, 1e30
CNT_ROWS = 128
I16 = jnp.int16
I16_MIN = -2 ** 15
POS_TOP = 2 ** 15 - 1
UNKNOWN_COUNT = 2 ** 30
LOG2E = 1.4426950408889634
INT_MIN = -2 ** 31
NEG_BIG = -1e30


def _rms(x, g):
    return x * lax.rsqrt(jnp.mean(x * x, axis=-1, keepdims=True) + NORM_EPS) * g


def _dot(a, b):
    return jnp.dot(a, b, preferred_element_type=F32)


def _dot_nt(a, b):
    return lax.dot_general(a, b, (((1,), (1,)), ((), ())), preferred_element_type=F32)


def _split3(x):
    hi = x.astype(BF16)
    r1 = x - hi.astype(F32)
    mid = r1.astype(BF16)
    lo = (r1 - mid.astype(F32)).astype(BF16)
    return hi, mid, lo


def _dot_seg(x, m):
    hi, lo = _split2(x)
    return _dot(hi, m) + _dot(lo, m)


def _dot_exact_lhs(m, x):
    hi, mid, lo = _split3(x)
    return _dot(m, hi) + _dot(m, mid) + _dot(m, lo)


def _split2(x):
    hi = x.astype(BF16)
    lo = (x - hi.astype(F32)).astype(BF16)
    return hi, lo


def _mm3(a, b):
    ah, al = _split2(a)
    bh, bl = _split2(b)
    return _dot(ah, bh) + _dot(ah, bl) + _dot(al, bh)


def _mm3_nt(a, b):
    ah, al = _split2(a)
    bh, bl = _split2(b)
    return _dot_nt(ah, bh) + _dot_nt(ah, bl) + _dot_nt(al, bh)


def _ffn_body(x_ref, g_ref, wg_ref, wu_ref, wd_ref, fg_ref, o_ref, xn_ref, acc_ref, *, final_norm):
    j = pl.program_id(1)

    @pl.when(j == 0)
    def _():
        xn_ref[...] = _rms(x_ref[...], g_ref[...]).astype(BF16)
        acc_ref[...] = jnp.zeros_like(acc_ref)

    xn = xn_ref[...]
    gt = _dot(xn, wg_ref[...])
    ut = _dot(xn, wu_ref[...])
    act = (gt * jax.nn.sigmoid(gt) * ut).astype(BF16)
    acc_ref[...] += _dot(act, wd_ref[...])

    @pl.when(j == pl.num_programs(1) - 1)
    def _():
        h = x_ref[...] + 0.5 * acc_ref[...]
        if final_norm:
            h = _rms(h, fg_ref[...])
        o_ref[...] = h


def _ffn(x, gain, wg, wu, wd, fgain, *, final_norm, tm=512, tf=512):
    s, d = x.shape
    f = wg.shape[1]
    return pl.pallas_call(
        functools.partial(_ffn_body, final_norm=final_norm),
        name="ffn_final" if final_norm else "ffn",
        grid=(s // tm, f // tf),
        in_specs=[
            pl.BlockSpec((tm, d), lambda i, j: (i, 0)),
            pl.BlockSpec((1, d), lambda i, j: (0, 0)),
            pl.BlockSpec((d, tf), lambda i, j: (0, j)),
            pl.BlockSpec((d, tf), lambda i, j: (0, j)),
            pl.BlockSpec((tf, d), lambda i, j: (j, 0)),
            pl.BlockSpec((1, d), lambda i, j: (0, 0)),
        ],
        out_specs=pl.BlockSpec((tm, d), lambda i, j: (i, 0)),
        out_shape=jax.ShapeDtypeStruct((s, d), F32),
        scratch_shapes=[pltpu.VMEM((tm, d), BF16), pltpu.VMEM((tm, d), F32)],
        compiler_params=pltpu.CompilerParams(
            dimension_semantics=("parallel", "arbitrary"), vmem_limit_bytes=VMEM_LIMIT),
    )(x, gain, wg, wu, wd, fgain)


def _inproj_body(x_ref, g_ref, w_ref, o_ref, xn_ref):
    @pl.when(pl.program_id(1) == 0)
    def _():
        xn_ref[...] = _rms(x_ref[...], g_ref[...]).astype(BF16)

    o_ref[...] = _dot(xn_ref[...], w_ref[...])


def _inproj(h, gain, w, *, tm=1024, tn=512):
    s, d = h.shape
    n = w.shape[1]
    return pl.pallas_call(
        _inproj_body,
        name="inproj",
        grid=(s // tm, n // tn),
        in_specs=[
            pl.BlockSpec((tm, d), lambda i, j: (i, 0)),
            pl.BlockSpec((1, d), lambda i, j: (0, 0)),
            pl.BlockSpec((d, tn), lambda i, j: (0, j)),
        ],
        out_specs=pl.BlockSpec((tm, tn), lambda i, j: (i, j)),
        out_shape=jax.ShapeDtypeStruct((s, n), F32),
        scratch_shapes=[pltpu.VMEM((tm, d), BF16)],
        compiler_params=pltpu.CompilerParams(
            dimension_semantics=("parallel", "arbitrary"), vmem_limit_bytes=VMEM_LIMIT),
    )(h, gain, w)


def _rope_tables(s, head_dim):
    rot = head_dim // ROPE_FRACTION
    half = rot // 2
    inv_freq = ROPE_THETA ** (-(jnp.arange(half, dtype=F32) * 2.0 / rot))
    ang = jnp.arange(s, dtype=F32)[:, None] * inv_freq[None, :]
    cos, sin = jnp.cos(ang), jnp.sin(ang)
    ones = jnp.ones((s, head_dim - rot), F32)
    zeros = jnp.zeros((s, head_dim - rot), F32)
    zh = jnp.zeros((s, half), F32)
    c = jnp.concatenate([cos, cos, ones], axis=1)
    sp = jnp.concatenate([zh, sin, zeros], axis=1)
    sm = jnp.concatenate([-sin, zh, zeros], axis=1)
    reps = LANES // head_dim
    return tuple(jnp.tile(t, (1, reps)) for t in (c, sp, sm))


def _rope128(x, c, sp, sm, half):
    return x * c + pltpu.roll(x, half, 1) * sp + pltpu.roll(x, LANES - half, 1) * sm


def _attn_prep_body(q_ref, kv_ref, qi_ref, misc_ref, c1_ref, sp1_ref, sm1_ref, c2_ref, sp2_ref, sm2_ref,
                    qo_ref, ko_ref, vo_ref, qio_ref, kio_ref, wio_ref):
    c1, sp1, sm1 = c1_ref[...], sp1_ref[...], sm1_ref[...]
    c2, sp2, sm2 = c2_ref[...], sp2_ref[...], sm2_ref[...]
    att_scale = ATTN_HEAD_DIM ** -0.5 * LOG2E
    for b in range(ATTN_WIDTH // LANES):
        sl = slice(b * LANES, (b + 1) * LANES)
        qo_ref[:, sl] = (_rope128(q_ref[:, sl], c1, sp1, sm1, 16) * att_scale).astype(BF16)
    for b in range(KV_WIDTH // LANES):
        sl = slice(b * LANES, (b + 1) * LANES)
        ko_ref[:, sl] = _rope128(kv_ref[:, sl], c1, sp1, sm1, 16).astype(BF16)
    vt = kv_ref[:, KV_WIDTH:].T.astype(BF16)
    ones = jnp.ones((VT_ROWS - ATTN_HEAD_DIM, vt.shape[1]), BF16)
    for g in range(ATTN_KV_HEADS):
        vo_ref[0, g * VT_ROWS:g * VT_ROWS + ATTN_HEAD_DIM, :] = vt[g * ATTN_HEAD_DIM:(g + 1) * ATTN_HEAD_DIM]
        vo_ref[0, g * VT_ROWS + ATTN_HEAD_DIM:(g + 1) * VT_ROWS, :] = ones
    for b in range(IDX_HEADS * IDX_HEAD_DIM // LANES):
        sl = slice(b * LANES, (b + 1) * LANES)
        qio_ref[:, sl] = _rope128(qi_ref[:, sl], c2, sp2, sm2, 8).astype(BF16)
    ki = _rope128(misc_ref[:, :LANES], c2, sp2, sm2, 8)
    kio_ref[...] = ki[:, :IDX_HEAD_DIM].astype(BF16)
    idx_scale = (IDX_HEADS ** -0.5) * (IDX_HEAD_DIM ** -0.5)
    wio_ref[...] = (misc_ref[:, 2 * LANES:3 * LANES] * idx_scale).T[:IDX_HEADS]


def _attn_prep(u, tabs128, tabs64, *, tm):
    s = u.shape[0]
    tab_spec = pl.BlockSpec((tm, LANES), lambda i: (i, 0))
    return pl.pallas_call(
        _attn_prep_body,
        name="attn_prep",
        grid=(s // tm,),
        in_specs=[
            pl.BlockSpec((tm, ATTN_WIDTH), lambda i: (i, U_Q // ATTN_WIDTH)),
            pl.BlockSpec((tm, 512), lambda i: (i, U_KV // 512)),
            pl.BlockSpec((tm, 512), lambda i: (i, U_QI // 512)),
            pl.BlockSpec((tm, 512), lambda i: (i, U_MISC // 512)),
        ] + [tab_spec] * 6,
        out_specs=[
            pl.BlockSpec((tm, ATTN_WIDTH), lambda i: (i, 0)),
            pl.BlockSpec((tm, KV_WIDTH), lambda i: (i, 0)),
            pl.BlockSpec((1, ATTN_KV_HEADS * VT_ROWS, tm), lambda i: (i, 0, 0)),
            pl.BlockSpec((tm, 512), lambda i: (i, 0)),
            pl.BlockSpec((tm, IDX_HEAD_DIM), lambda i: (i, 0)),
            pl.BlockSpec((IDX_HEADS, tm), lambda i: (0, i)),
        ],
        out_shape=[
            jax.ShapeDtypeStruct((s, ATTN_WIDTH), BF16),
            jax.ShapeDtypeStruct((s, KV_WIDTH), BF16),
            jax.ShapeDtypeStruct((s // tm, ATTN_KV_HEADS * VT_ROWS, tm), BF16),
            jax.ShapeDtypeStruct((s, 512), BF16),
            jax.ShapeDtypeStruct((s, IDX_HEAD_DIM), BF16),
            jax.ShapeDtypeStruct((IDX_HEADS, s), F32),
        ],
        compiler_params=pltpu.CompilerParams(
            dimension_semantics=("parallel",), vmem_limit_bytes=VMEM_LIMIT),
    )(u, u, u, u, *tabs128, *tabs64)


def _col_reduce(x, op):
    n, w = x.shape
    y = x.reshape(n // 64, 64, w)
    acc = y[0]
    for a in range(1, n // 64):
        acc = op(acc, y[a])
    z = acc.reshape(8, 8, w)
    t = [op(z[2 * i], z[2 * i + 1]) for i in range(4)]
    r = op(op(t[0], t[1]), op(t[2], t[3]))
    red = jnp.max if op is jnp.maximum else jnp.sum
    return red(r, axis=0, keepdims=True)


def _dsa_body(q_ref, qi_ref, wt_ref, k_ref, vt_ref, ki_ref, o_ref,
              key_ref, hi_ref, lo_ref, qs_ref, qis_ref, p1_ref, p2_ref, r1_ref, r2_ref, run_ref, accr_ref,
              flag_ref, s_ref, s2_ref, smax_ref, smax2_ref, m_ref, acc_ref, *, tq, tk, k_sel, pos_span):
    qb = pl.program_id(0)
    q_end = (qb + 1) * tq
    n_kt = (q_end + tk - 1) // tk
    n_full = (qb * tq) // tk
    n_pair = ATTN_HEADS // 2

    for h in range(ATTN_HEADS):
        qs_ref[h * tq:(h + 1) * tq, :] = q_ref[:, h * ATTN_HEAD_DIM:(h + 1) * ATTN_HEAD_DIM]
    for h in range(IDX_HEADS):
        qis_ref[h * tq:(h + 1) * tq, :] = qi_ref[:, h * IDX_HEAD_DIM:(h + 1) * IDX_HEAD_DIM]

    qpos = lax.broadcasted_iota(I32, (1, tq), 1) + qb * tq
    key_lim = (qpos // CHUNK + 1) * CHUNK
    wt = wt_ref[...]

    def score_tile(jt, masked):
        kit = ki_ref[pl.ds(pl.multiple_of(jt * tk, tk), tk), :]
        acc = jnp.zeros((tk, tq), F32)
        for p in range(IDX_HEADS // 2):
            s2 = _dot_nt(kit, qis_ref[2 * p * tq:(2 * p + 2) * tq, :])
            acc = acc + jnp.maximum(s2[:, :tq], 0.0) * wt[2 * p:2 * p + 1, :]
            acc = acc + jnp.maximum(s2[:, tq:], 0.0) * wt[2 * p + 1:2 * p + 2, :]
        bits = pltpu.bitcast(acc, I32)
        key = bits ^ ((bits >> 31) & 0x7FFFFFFF)
        if masked:
            kpos = lax.broadcasted_iota(I32, (tk, tq), 0) + jt * tk
            key = jnp.where(kpos < key_lim, key, INT_MIN)
        key_ref[jt] = key
        hi_ref[jt] = (key >> 16).astype(I16)
        lo_ref[jt] = (key ^ 0x8000).astype(I16)

    def two_full_tiles(i, c):
        score_tile(2 * i, False)
        score_tile(2 * i + 1, False)
        return c

    def diag_tile(jt, c):
        score_tile(jt, True)
        return c

    lax.fori_loop(0, n_full // 2, two_full_tiles, 0)
    lax.fori_loop(n_full - n_full % 2, n_kt, diag_tile, 0)

    @pl.when(n_kt % 2 == 1)
    def _():
        key_ref[n_kt] = jnp.full((tk, tq), INT_MIN, I32)

    one16 = jnp.ones((tk, tq), I16)
    zero16 = jnp.zeros((tk, tq), I16)

    def fold(cnt, hit):
        hit = hit.reshape(tk // CNT_ROWS, CNT_ROWS, tq)
        for a in range(tk // CNT_ROWS):
            cnt = cnt + hit[a]
        return cnt

    def total(cnt):
        return jnp.sum(cnt.astype(I32), axis=0, keepdims=True)

    def search16(half_ref, need, start=I16_MIN, bits=16):
        def bit_step(i, carry):
            base, got = carry
            cand = base + lax.shift_left(jnp.int32(1), bits - 1 - i)
            cand16 = cand.astype(I16)

            def count_tile(jt, cnt):
                return fold(cnt, jnp.where(half_ref[jt] >= cand16, one16, zero16))

            tot = total(lax.fori_loop(0, n_kt, count_tile, jnp.zeros((CNT_ROWS, tq), I16)))
            ok = tot >= need
            return jnp.where(ok, cand, base), jnp.where(ok, tot, got)

        init = (jnp.full((1, tq), start, I32), jnp.full((1, tq), UNKNOWN_COUNT, I32))
        return lax.fori_loop(0, bits, bit_step, init)

    hi_thr, _ = search16(hi_ref, k_sel)
    hi_thr16 = hi_thr.astype(I16)

    def bucket_tile(jt, cnt):
        hi = hi_ref[jt]
        lo_ref[jt] = jnp.where(hi == hi_thr16, lo_ref[jt], jnp.int16(I16_MIN))
        return fold(cnt, jnp.where(hi > hi_thr16, one16, zero16))

    above = total(lax.fori_loop(0, n_kt, bucket_tile, jnp.zeros((CNT_ROWS, tq), I16)))
    lo_thr, reach = search16(lo_ref, k_sel - above)
    thr_raw = hi_thr * 65536 + (lo_thr + 32768)
    thr = jnp.maximum(thr_raw, INT_MIN + 1)

    maybe_tie = ((above + reach) > k_sel) & (thr_raw > INT_MIN)

    @pl.when(jnp.max(jnp.where(maybe_tie, 1, 0)) > 0)
    def _():
        rows = lax.broadcasted_iota(I32, (tk, tq), 0)
        rows16 = rows.astype(I16)
        lo_thr16 = lo_thr.astype(I16)

        def earliness_tile(jt, cnt):
            lo = lo_ref[jt]
            tied = (hi_ref[jt] == hi_thr16) & (lo == lo_thr16)
            hi_ref[jt] = jnp.where(tied, (POS_TOP - jt * tk).astype(I16) - rows16, jnp.int16(I16_MIN))
            return fold(cnt, jnp.where(lo > lo_thr16, one16, zero16))

        beyond = above + total(lax.fori_loop(0, n_kt, earliness_tile, jnp.zeros((CNT_ROWS, tq), I16)))
        keep = jnp.where(maybe_tie, k_sel - beyond, UNKNOWN_COUNT)
        first_kept, _ = search16(hi_ref, keep, start=POS_TOP + 1 - pos_span, bits=pos_span.bit_length() - 1)
        last = POS_TOP - first_kept

        def retire_tile(jt, c):
            key = key_ref[jt]
            drop = (key == thr_raw) & (rows + jt * tk > last) & maybe_tie
            key_ref[jt] = jnp.where(drop, INT_MIN, key)
            return c

        lax.fori_loop(0, n_kt, retire_tile, 0)

    def masked_scores(jt, p, sel2):
        g = p // (n_pair // ATTN_KV_HEADS)
        off = pl.multiple_of(jt * tk, tk)
        kg = k_ref[pl.ds(off, tk), g * ATTN_HEAD_DIM:(g + 1) * ATTN_HEAD_DIM]
        s = _dot_nt(kg, qs_ref[2 * p * tq:(2 * p + 2) * tq, :])
        return s if sel2 is None else jnp.where(sel2, s, NEG_BIG)

    def selection(jt):
        sel = key_ref[jt] >= thr
        return jnp.concatenate([sel, sel], axis=1)

    def values(jt, p):
        g = p // (n_pair // ATTN_KV_HEADS)
        return vt_ref[jt, g * VT_ROWS:(g + 1) * VT_ROWS, :]

    def two_tiles_loop(stage, fold_in, bufs):
        stage(bufs[0], 0)

        def trip(i, c):
            j0 = 2 * i
            stage(bufs[1], j0 + 1)
            fold_in(bufs[0], j0)
            stage(bufs[0], jnp.minimum(j0 + 2, n_kt - 1))
            fold_in(bufs[1], j0 + 1)
            return c

        lax.fori_loop(0, (n_kt + 1) // 2, trip, 0)

    for p in range(n_pair):
        r0 = _col_reduce(masked_scores(0, p, None), jnp.maximum)
        run_ref[p] = r0
        accr_ref[p] = r0
    flag_ref[...] = jnp.full(flag_ref.shape, NEG_BIG, F32)
    acc_ref[...] = jnp.zeros(acc_ref.shape, F32)

    def stage_probs(buf, jt):
        p_ref, r_ref = buf
        sel2 = selection(jt)
        for p in range(n_pair):
            s = masked_scores(jt, p, sel2)
            r = run_ref[p]
            part = None
            for c0 in range(0, tk, 64):
                sc = s[c0:c0 + 64]
                p_ref[p, c0:c0 + 64, :] = jnp.exp2(sc - r).astype(BF16)
                part = sc if part is None else jnp.maximum(part, sc)
            tmax = _col_reduce(part, jnp.maximum)
            r_ref[p] = r
            flag_ref[p] = jnp.maximum(flag_ref[p], tmax - r)
            run_ref[p] = jnp.maximum(r, tmax)

    def fold_probs(buf, jt):
        p_ref, r_ref = buf
        for p in range(n_pair):
            r = r_ref[p]
            acc_ref[p] = acc_ref[p] * jnp.exp2(accr_ref[p] - r) + _dot(values(jt, p), p_ref[p])
            accr_ref[p] = r

    two_tiles_loop(stage_probs, fold_probs, ((p1_ref, r1_ref), (p2_ref, r2_ref)))

    bad = jnp.float32(0.0)
    for p in range(n_pair):
        den = acc_ref[p][ATTN_HEAD_DIM:ATTN_HEAD_DIM + 1]
        ok = (flag_ref[p] < EXP2_HEADROOM) & (den > DEN_MIN) & (den < DEN_MAX)
        bad = jnp.maximum(bad, jnp.max(jnp.where(ok, 0.0, 1.0)))

    @pl.when(bad > 0.0)
    def _():
        m_ref[...] = jnp.full(m_ref.shape, NEG_BIG, F32)
        acc_ref[...] = jnp.zeros(acc_ref.shape, F32)

        def stage_scores(buf, jt):
            s_buf, max_buf = buf
            sel2 = selection(jt)
            for p in range(n_pair):
                s = masked_scores(jt, p, sel2)
                s_buf[p] = s
                max_buf[p] = _col_reduce(s, jnp.maximum)

        def fold_scores(buf, jt):
            s_buf, max_buf = buf
            for p in range(n_pair):
                m_prev = m_ref[p]
                m_cur = jnp.maximum(m_prev, max_buf[p])
                pr = jnp.exp2(s_buf[p] - m_cur).astype(BF16)
                acc_ref[p] = acc_ref[p] * jnp.exp2(m_prev - m_cur) + _dot(values(jt, p), pr)
                m_ref[p] = m_cur

        two_tiles_loop(stage_scores, fold_scores, ((s_ref, smax_ref), (s2_ref, smax2_ref)))

    for p in range(n_pair):
        acc = acc_ref[p]
        o = acc[:ATTN_HEAD_DIM] / acc[ATTN_HEAD_DIM:ATTN_HEAD_DIM + 1]
        for r in range(2):
            h = 2 * p + r
            o_ref[:, h * ATTN_HEAD_DIM:(h + 1) * ATTN_HEAD_DIM] = o[:, r * tq:(r + 1) * tq].T.astype(BF16)


def _dsa(q, qi, wt, k, vt, ki, *, tq=128):
    s = q.shape[0]
    tk = vt.shape[2]
    k_sel = min(TOPK_MAX, s // 4)
    assert s <= POS_TOP + 1 and (s // tk) % 2 == 0 and s % tq == 0
    whole = lambda shape: pl.BlockSpec(shape, lambda i: (0,) * len(shape), pipeline_mode=pl.Buffered(1))
    return pl.pallas_call(
        functools.partial(_dsa_body, tq=tq, tk=tk, k_sel=k_sel, pos_span=1 << (s - 1).bit_length()),
        name="dsa",
        grid=(s // tq,),
        in_specs=[
            pl.BlockSpec((tq, ATTN_WIDTH), lambda i: (i, 0)),
            pl.BlockSpec((tq, 512), lambda i: (i, 0)),
            pl.BlockSpec((IDX_HEADS, tq), lambda i: (0, i)),
            whole((s, KV_WIDTH)),
            whole((s // tk, ATTN_KV_HEADS * VT_ROWS, tk)),
            whole((s, IDX_HEAD_DIM)),
        ],
        out_specs=pl.BlockSpec((tq, ATTN_WIDTH), lambda i: (i, 0)),
        out_shape=jax.ShapeDtypeStruct((s, ATTN_WIDTH), BF16),
        scratch_shapes=[
            pltpu.VMEM((s // tk, tk, tq), I32),
            pltpu.VMEM((s // tk, tk, tq), I16),
            pltpu.VMEM((s // tk, tk, tq), I16),
            pltpu.VMEM((ATTN_HEADS * tq, ATTN_HEAD_DIM), BF16),
            pltpu.VMEM((IDX_HEADS * tq, IDX_HEAD_DIM), BF16),
            pltpu.VMEM((ATTN_HEADS // 2, tk, 2 * tq), BF16),
            pltpu.VMEM((ATTN_HEADS // 2, tk, 2 * tq), BF16),
            pltpu.VMEM((ATTN_HEADS // 2, 1, 2 * tq), F32),
            pltpu.VMEM((ATTN_HEADS // 2, 1, 2 * tq), F32),
            pltpu.VMEM((ATTN_HEADS // 2, 1, 2 * tq), F32),
            pltpu.VMEM((ATTN_HEADS // 2, 1, 2 * tq), F32),
            pltpu.VMEM((ATTN_HEADS // 2, 1, 2 * tq), F32),
            pltpu.VMEM((ATTN_HEADS // 2, tk, 2 * tq), F32),
            pltpu.VMEM((ATTN_HEADS // 2, tk, 2 * tq), F32),
            pltpu.VMEM((ATTN_HEADS // 2, 1, 2 * tq), F32),
            pltpu.VMEM((ATTN_HEADS // 2, 1, 2 * tq), F32),
            pltpu.VMEM((ATTN_HEADS // 2, 1, 2 * tq), F32),
            pltpu.VMEM((ATTN_HEADS // 2, VT_ROWS, 2 * tq), F32),
        ],
        compiler_params=pltpu.CompilerParams(
            dimension_semantics=("parallel",), vmem_limit_bytes=VMEM_LIMIT),
    )(q, qi, wt, k, vt, ki)


def _token_shift(x, prev8, first):
    rolled = pltpu.roll(x, 1, 0)
    prev_row = jnp.where(first, 0.0, prev8[7:8, :])
    row = lax.broadcasted_iota(I32, x.shape, 0)
    return jnp.where(row == 0, jnp.broadcast_to(prev_row, x.shape), rolled)


def _rwkv_prep_body(rr_ref, rk_ref, rv_ref, ms_ref, prr_ref, prk_ref, prv_ref, pms_ref,
                    mur_ref, muk_ref, muv_ref, mum_ref, w0_ref, w2_ref, a0_ref, a2_ref, g2_ref,
                    kk_ref, ka_ref, rkk_ref, bd_ref, tri_ref, blk_ref,
                    mc_ref, nc_ref, yc_ref, y0_ref, bonus_ref, g_ref,
                    at_s, rt_s, bt_s, kt_s, v_s, bdt_s, kdt_s, pct_s):
    first = pl.program_id(0) == 0
    tm = rr_ref.shape[0]

    def lerp(x_ref, p_ref, mu_ref):
        x = x_ref[...]
        return x + (_token_shift(x, p_ref[...], first) - x) * mu_ref[...]

    r = lerp(rr_ref, prr_ref, mur_ref)
    k = lerp(rk_ref, prk_ref, muk_ref)
    v = lerp(rv_ref, prv_ref, muv_ref)
    ms = lerp(ms_ref, pms_ref, mum_ref)
    xw = ms[:, :LANES]
    xag = ms[:, LANES:]

    w = w0_ref[...] + _dot(jnp.tanh(xw).astype(BF16), w2_ref[...])
    z = -w
    softplus = jnp.maximum(z, 0.0) + jnp.log1p(jnp.exp(-jnp.abs(z)))
    lw = -jnp.exp(-softplus - 0.5)
    a_lr = jax.nn.sigmoid(a0_ref[...] + _dot(xag.astype(BF16), a2_ref[...]))
    g_ref[...] = _dot(jax.nn.sigmoid(xag).astype(BF16), g2_ref[...])

    bd = bd_ref[...]
    kk = k * kk_ref[...]
    nrm = jnp.sqrt(_dot_seg(kk * kk, bd))
    kk = kk / jnp.maximum(nrm, 1e-12)
    k = k * (1.0 + (a_lr - 1.0) * ka_ref[...])
    bonus_ref[...] = _dot_seg(r * k * rkk_ref[...], bd) * v

    cum = _dot_exact_lhs(tri_ref[...], lw)
    tot = _dot_exact_lhs(blk_ref[...], lw)
    e_neg = jnp.exp(-cum)
    e_dec = jnp.exp(tot - cum)
    b = kk * a_lr
    gw = RW_GROUP_LANES
    groups = RWKV_WIDTH // gw
    time_major = ((at_s, -kk * jnp.exp(cum - lw)), (rt_s, r * jnp.exp(cum)), (bt_s, b * e_neg),
                  (kt_s, k * e_neg), (v_s, v))
    chan_major = ((bdt_s, (b * e_dec).T), (kdt_s, (k * e_dec).T), (pct_s, jnp.exp(tot).T))
    for g in range(groups):
        for ref, val in time_major:
            ref[g] = val[:, g * gw:(g + 1) * gw]
        for ref, val in chan_major:
            ref[g] = val[g * gw:(g + 1) * gw, :]

    def group_maps(g, c):
        _chunk_maps(at_s.at[g], rt_s.at[g], bt_s.at[g], kt_s.at[g], v_s.at[g],
                    bdt_s.at[g], kdt_s.at[g], pct_s.at[g],
                    mc_ref.at[g], nc_ref.at[g], yc_ref.at[g], y0_ref.at[g],
                    heads=gw // RWKV_HEAD_DIM, chunks=tm // RW_CHUNK)
        return c

    lax.fori_loop(0, groups, group_maps, 0)


def _rwkv_prep(u, p, *, tm=512):
    s = u.shape[0]
    w = RWKV_WIDTH
    t8 = tm // 8
    tile = lambda col: pl.BlockSpec((tm, w), lambda i: (i, col // w))
    prev = lambda col: pl.BlockSpec((8, w), lambda i: (jnp.maximum(i * t8 - 1, 0), col // w))
    vec = lambda n: pl.BlockSpec((1, n), lambda i: (0, 0))
    mat = lambda a, b: pl.BlockSpec((a, b), lambda i: (0, 0))
    tm_spec = pl.BlockSpec((tm, w), lambda i: (i, 0))
    gw = RW_GROUP_LANES
    groups = w // gw
    map_spec = pl.BlockSpec((groups, tm, gw), lambda i: (0, i, 0))
    outs = pl.pallas_call(
        _rwkv_prep_body,
        name="rwkv_prep",
        grid=(s // tm,),
        in_specs=[
            tile(U_RR), tile(U_RK), tile(U_RV),
            pl.BlockSpec((tm, 256), lambda i: (i, U_MISC // 256)),
            prev(U_RR), prev(U_RK), prev(U_RV),
            pl.BlockSpec((8, 256), lambda i: (jnp.maximum(i * t8 - 1, 0), U_MISC // 256)),
            vec(w), vec(w), vec(w), vec(256),
            vec(w), mat(LANES, w), vec(w), mat(LANES, w), mat(LANES, w),
            vec(w), vec(w), vec(w),
            mat(w, w), mat(tm, tm), mat(tm, tm),
        ],
        out_specs=[map_spec] * 4 + [tm_spec] * 2,
        out_shape=[jax.ShapeDtypeStruct((groups, s, gw), F32)] * 4 + [jax.ShapeDtypeStruct((s, w), F32)] * 2,
        scratch_shapes=[pltpu.VMEM((groups, tm, gw), F32)] * 5 + [pltpu.VMEM((groups, gw, tm), F32)] * 3,
        compiler_params=pltpu.CompilerParams(
            dimension_semantics=("parallel",), vmem_limit_bytes=VMEM_LIMIT),
    )(u, u, u, u, u, u, u, u,
      p["mu_r"], p["mu_k"], p["mu_v"], p["mu_m"], p["w0"], p["w2"], p["a0"], p["a2"], p["g2"],
      p["k_k"], p["k_a"], p["r_k"], p["bd"], p["tri"], p["blk"])
    return outs


def _mm1(a, b):
    return _dot(a.astype(BF16), b.astype(BF16))


def _chunk_maps(at_ref, rt_ref, bt_ref, kt_ref, v_ref, bdt_ref, kdt_ref, pct_ref,
                mc_ref, nc_ref, yc_ref, y0_ref, *, heads, chunks):
    n = RWKV_HEAD_DIM
    cl = RW_CHUNK
    ri = lax.broadcasted_iota(I32, (cl, cl), 0)
    ci = lax.broadcasted_iota(I32, (cl, cl), 1)
    strict = ri > ci
    incl = ri >= ci
    same16 = (ri // 16) == (ci // 16)
    eye = (ri == ci).astype(F32)

    pairs = [(slice(cc * cl, (cc + 1) * cl), slice(hh * n, (hh + 1) * n))
             for hh in range(heads) for cc in range(chunks)]
    each = lambda fn, *lists: [fn(*xs) for xs in zip(*lists)]

    at = [at_ref[ts, ls] for ts, ls in pairs]
    rt = [rt_ref[ts, ls] for ts, ls in pairs]
    v = [v_ref[ts, ls] for ts, ls in pairs]
    g = [_mm3_nt(jnp.concatenate([a, r], axis=0), jnp.concatenate([bt_ref[ts, ls], kt_ref[ts, ls]], axis=0))
         for a, r, (ts, ls) in zip(at, rt, pairs)]
    l = [jnp.where(strict, x[:cl, :cl], 0.0) for x in g]
    a_ak = [jnp.where(strict, x[:cl, cl:], 0.0) for x in g]
    a_rb = [jnp.where(incl, x[cl:, :cl], 0.0) for x in g]
    a_rk = [jnp.where(incl, x[cl:, cl:], 0.0) for x in g]

    ld = [jnp.where(same16, x, 0.0) for x in l]
    lo = each(lambda x, y: x - y, l, ld)
    d = [eye + x for x in ld]
    pw = ld
    for _ in range(3):
        pw = each(_mm1, pw, pw)
        d = each(lambda x, y: x + _mm1(x, y), d, pw)
    m = each(_mm1, d, lo)
    m2 = each(_mm1, m, m)
    f = each(lambda x, y: x + _mm1(y, x), d, m2)
    t = each(lambda x, y: x + _mm1(y, x), f, m)

    w1 = each(_mm1, a_ak, v)
    x = each(lambda tt, a, w: _mm1(tt, jnp.concatenate([a, w], axis=1)), t, at, w1)
    big = [_mm1(jnp.concatenate([bdt_ref[ls, ts], arb], axis=0), xx)
           for arb, xx, (ts, ls) in zip(a_rb, x, pairs)]
    kv = [_mm1(jnp.concatenate([kdt_ref[ls, ts], ark], axis=0), vv)
          for ark, vv, (ts, ls) in zip(a_rk, v, pairs)]
    for i, (ts, ls) in enumerate(pairs):
        mc_ref[ts, ls] = eye * pct_ref[ls, ts] + big[i][:n, :n]
        nc_ref[ts, ls] = big[i][:n, n:] + kv[i][:n]
        yc_ref[ts, ls] = rt[i] + big[i][n:, :n]
        y0_ref[ts, ls] = big[i][n:, n:] + kv[i][n:]


def _rwkv_state_body(mc_ref, nc_ref, yc_ref, y0_ref, y_ref, st_ref, *, chunks):
    n = RWKV_HEAD_DIM
    cl = RW_CHUNK
    per_group = RW_GROUP_LANES // n

    @pl.when(pl.program_id(0) == 0)
    def _():
        st_ref[...] = jnp.zeros_like(st_ref)

    heads = [(hh // per_group, slice((hh % per_group) * n, (hh % per_group + 1) * n), slice(hh * n, (hh + 1) * n))
             for hh in range(RWKV_HEADS)]
    st = [st_ref[:, ls] for _, _, ls in heads]
    for cc in range(chunks):
        ts = slice(cc * cl, (cc + 1) * cl)
        r = [_mm3(jnp.concatenate([mc_ref[g, ts, gl], yc_ref[g, ts, gl]], axis=0), s)
             for (g, gl, _), s in zip(heads, st)]
        for (g, gl, ls), x in zip(heads, r):
            y_ref[ts, ls] = x[n:] + y0_ref[g, ts, gl]
        st = [x[:n] + nc_ref[g, ts, gl] for (g, gl, _), x in zip(heads, r)]
    for (_, _, ls), s in zip(heads, st):
        st_ref[:, ls] = s


def _rwkv_state(mc, nc, yc, y0, *, chunks=4):
    groups, s, gw = mc.shape
    w = groups * gw
    rows = chunks * RW_CHUNK
    blk = pl.BlockSpec((groups, rows, gw), lambda c: (0, c, 0))
    return pl.pallas_call(
        functools.partial(_rwkv_state_body, chunks=chunks),
        name="rwkv_state",
        grid=(s // rows,),
        in_specs=[blk] * 4,
        out_specs=pl.BlockSpec((rows, w), lambda c: (c, 0)),
        out_shape=jax.ShapeDtypeStruct((s, w), F32),
        scratch_shapes=[pltpu.VMEM((RWKV_HEAD_DIM, w), F32)],
        compiler_params=pltpu.CompilerParams(
            dimension_semantics=("arbitrary",), vmem_limit_bytes=VMEM_LIMIT),
    )(mc, nc, yc, y0)


def _outproj_body(h_ref, attn_ref, y_ref, bonus_ref, g_ref, gain_ref, bias_ref, bd_ref, wa_ref, wr_ref, o_ref):
    y = y_ref[...]
    bd = bd_ref[...]
    inv_n = 1.0 / RWKV_HEAD_DIM
    mean = _dot_seg(y, bd) * inv_n
    yc = y - mean
    var = _dot_seg(yc * yc, bd) * inv_n
    yn = yc * lax.rsqrt(var + GN_EPS) * gain_ref[...] + bias_ref[...]
    rw = ((yn + bonus_ref[...]) * g_ref[...]).astype(BF16)
    o_ref[...] = h_ref[...] + _dot(attn_ref[...], wa_ref[...]) + _dot(rw, wr_ref[...])


def _outproj(h, attn, y, bonus, g, gain, bias, bd, wa, wr, *, tm=256):
    s, d = h.shape
    w = RWKV_WIDTH
    row = lambda n: pl.BlockSpec((tm, n), lambda i: (i, 0))
    whole = lambda a, b: pl.BlockSpec((a, b), lambda i: (0, 0))
    return pl.pallas_call(
        _outproj_body,
        name="outproj",
        grid=(s // tm,),
        in_specs=[row(d), row(w), row(w), row(w), row(w), whole(1, w), whole(1, w), whole(w, w),
                  whole(ATTN_WIDTH, d), whole(w, d)],
        out_specs=row(d),
        out_shape=jax.ShapeDtypeStruct((s, d), F32),
        compiler_params=pltpu.CompilerParams(
            dimension_semantics=("parallel",), vmem_limit_bytes=VMEM_LIMIT),
    )(h, attn, y, bonus, g, gain, bias, bd, wa, wr)


def _relayout_w_in(w_in):
    d = w_in.shape[0]
    o = np.cumsum([0, 1024, 256, 256, 512, 64, 8, 1024, 1024, 1024, 64, 64, 64])
    seg = lambda i: w_in[:, o[i]:o[i + 1]]
    q, k, v, qi, ki, wi, rr, rk, rv, wd, ad, gd = (seg(i) for i in range(12))
    z = lambda n: jnp.zeros((d, n), w_in.dtype)
    cols = [q, k, v, qi, rr, rk, rv, ki, wd, ad, gd, wi, z(LANES - 8), z(LANES)]
    return jnp.concatenate(cols, axis=1).astype(BF16)


def _pad_rows(w2, lo):
    z = jnp.zeros_like(w2)
    return (jnp.concatenate([w2, z], axis=0) if lo == 0 else jnp.concatenate([z, w2], axis=0)).astype(BF16)


def kernel(x, ffn1_norm, ffn1_w_gate, ffn1_w_up, ffn1_w_down, mix_norm, w_in, w_out, rwkv_mu_r, rwkv_mu_k, rwkv_mu_v, rwkv_mu_w, rwkv_mu_a, rwkv_mu_g, rwkv_w0, rwkv_w2, rwkv_a0, rwkv_a2, rwkv_g2, rwkv_k_k, rwkv_k_a, rwkv_r_k, rwkv_gn_gain, rwkv_gn_bias, ffn2_norm, ffn2_w_gate, ffn2_w_up, ffn2_w_down, final_norm):
    b, s, d = x.shape
    assert b == 1 and d == D_MODEL and ffn1_norm.shape[0] == 1
    h = x[0]
    row = lambda a: a.reshape(1, -1).astype(F32)
    z64 = jnp.zeros((1, LORA), F32)

    tm_prep = 256
    tix = np.arange(tm_prep)
    same_chunk = (tix[:, None] // RW_CHUNK) == (tix[None, :] // RW_CHUNK)
    hix = np.arange(RWKV_WIDTH) // RWKV_HEAD_DIM
    prep = dict(
        mu_r=row(rwkv_mu_r[0]), mu_k=row(rwkv_mu_k[0]), mu_v=row(rwkv_mu_v[0]),
        mu_m=jnp.concatenate([z64, row(rwkv_mu_w[0]), row(rwkv_mu_a[0]), row(rwkv_mu_g[0])], axis=1),
        w0=row(rwkv_w0[0]), w2=_pad_rows(rwkv_w2[0], LORA),
        a0=row(rwkv_a0[0]), a2=_pad_rows(rwkv_a2[0], 0), g2=_pad_rows(rwkv_g2[0], LORA),
        k_k=row(rwkv_k_k[0]), k_a=row(rwkv_k_a[0]), r_k=row(rwkv_r_k[0]),
        bd=jnp.asarray(hix[:, None] == hix[None, :], BF16),
        tri=jnp.asarray(same_chunk & (tix[:, None] >= tix[None, :]), BF16),
        blk=jnp.asarray(same_chunk, BF16),
    )

    h1 = _ffn(h, row(ffn1_norm[0]), ffn1_w_gate[0].astype(BF16), ffn1_w_up[0].astype(BF16),
              ffn1_w_down[0].astype(BF16), row(final_norm), final_norm=False)
    u = _inproj(h1, row(mix_norm[0]), _relayout_w_in(w_in[0]))

    q, k, vt, qi, ki, wt = _attn_prep(u, _rope_tables(s, ATTN_HEAD_DIM), _rope_tables(s, IDX_HEAD_DIM),
                                     tm=DSA_KEY_TILE)
    attn = _dsa(q, qi, wt, k, vt, ki)

    mc, nc, yc, y0, bonus, g = _rwkv_prep(u, prep, tm=tm_prep)
    y = _rwkv_state(mc, nc, yc, y0)

    wo = w_out[0].astype(BF16)
    h2 = _outproj(h1, attn, y, bonus, g, row(rwkv_gn_gain[0]), row(rwkv_gn_bias[0]), prep["bd"],
                  wo[:ATTN_WIDTH], wo[ATTN_WIDTH:])
    out = _ffn(h2, row(ffn2_norm[0]), ffn2_w_gate[0].astype(BF16), ffn2_w_up[0].astype(BF16),
               ffn2_w_down[0].astype(BF16), row(final_norm), final_norm=True)
    return out[None]
```

```python
import functools

import jax
import jax.numpy as jnp
import numpy as np
from jax import lax
from jax.experimental import pallas as pl
from jax.experimental.pallas import tpu as pltpu

F32 = jnp.float32
BF16 = jnp.bfloat16
I32 = jnp.int32

D_MODEL = 2048
CHUNK = 64
ROPE_THETA = 500000.0
ROPE_FRACTION = 4
NORM_EPS = 1e-6
ATTN_HEAD_DIM = 128
ATTN_WIDTH = 1024
ATTN_HEADS = 8
ATTN_KV_HEADS = 2
KV_WIDTH = 256
IDX_HEADS = 8
IDX_HEAD_DIM = 64
TOPK_MAX = 256
RWKV_HEAD_DIM = 64
RWKV_WIDTH = 1024
RWKV_HEADS = 16
LORA = 64
GN_EPS = 64e-5
D_FF = 5632

LANES = 128
VMEM_LIMIT = 56 * 1024 * 1024

U_WIDTH = 5632
U_Q = 0
U_KV = 1024
U_QI = 1536
U_RR = 2048
U_RK = 3072
U_RV = 4096
U_MISC = 5120

RW_CHUNK = 64
RW_GROUP_LANES = 512
DSA_KEY_TILE = 512
VT_ROWS = ATTN_HEAD_DIM + 16
EXP2_HEADROOM = 60.0
DEN_MIN, DEN_MAX = 1e-30, 1e30
CNT_ROWS = 128
I16 = jnp.int16
I16_MIN = -2 ** 15
POS_TOP = 2 ** 15 - 1
UNKNOWN_COUNT = 2 ** 30
LOG2E = 1.4426950408889634
INT_MIN = -2 ** 31
NEG_BIG = -1e30


def _rms(x, g):
    return x * lax.rsqrt(jnp.mean(x * x, axis=-1, keepdims=True) + NORM_EPS) * g


def _dot(a, b):
    return jnp.dot(a, b, preferred_element_type=F32)


def _dot_nt(a, b):
    return lax.dot_general(a, b, (((1,), (1,)), ((), ())), preferred_element_type=F32)


def _split3(x):
    hi = x.astype(BF16)
    r1 = x - hi.astype(F32)
    mid = r1.astype(BF16)
    lo = (r1 - mid.astype(F32)).astype(BF16)
    return hi, mid, lo


def _dot_seg(x, m):
    hi, lo = _split2(x)
    return _dot(hi, m) + _dot(lo, m)


def _dot_exact_lhs(m, x):
    hi, mid, lo = _split3(x)
    return _dot(m, hi) + _dot(m, mid) + _dot(m, lo)


def _split2(x):
    hi = x.astype(BF16)
    lo = (x - hi.astype(F32)).astype(BF16)
    return hi, lo


def _mm3(a, b):
    ah, al = _split2(a)
    bh, bl = _split2(b)
    return _dot(ah, bh) + _dot(ah, bl) + _dot(al, bh)


def _mm3_nt(a, b):
    ah, al = _split2(a)
    bh, bl = _split2(b)
    return _dot_nt(ah, bh) + _dot_nt(ah, bl) + _dot_nt(al, bh)


def _ffn_body(x_ref, g_ref, wg_ref, wu_ref, wd_ref, fg_ref, o_ref, xn_ref, acc_ref, *, final_norm):
    j = pl.program_id(1)

    @pl.when(j == 0)
    def _():
        xn_ref[...] = _rms(x_ref[...], g_ref[...]).astype(BF16)
        acc_ref[...] = jnp.zeros_like(acc_ref)

    xn = xn_ref[...]
    gt = _dot(xn, wg_ref[...])
    ut = _dot(xn, wu_ref[...])
    act = (gt * jax.nn.sigmoid(gt) * ut).astype(BF16)
    acc_ref[...] += _dot(act, wd_ref[...])

    @pl.when(j == pl.num_programs(1) - 1)
    def _():
        h = x_ref[...] + 0.5 * acc_ref[...]
        if final_norm:
            h = _rms(h, fg_ref[...])
        o_ref[...] = h


def _ffn(x, gain, wg, wu, wd, fgain, *, final_norm, tm=512, tf=512):
    s, d = x.shape
    f = wg.shape[1]
    return pl.pallas_call(
        functools.partial(_ffn_body, final_norm=final_norm),
        name="ffn_final" if final_norm else "ffn",
        grid=(s // tm, f // tf),
        in_specs=[
            pl.BlockSpec((tm, d), lambda i, j: (i, 0)),
            pl.BlockSpec((1, d), lambda i, j: (0, 0)),
            pl.BlockSpec((d, tf), lambda i, j: (0, j)),
            pl.BlockSpec((d, tf), lambda i, j: (0, j)),
            pl.BlockSpec((tf, d), lambda i, j: (j, 0)),
            pl.BlockSpec((1, d), lambda i, j: (0, 0)),
        ],
        out_specs=pl.BlockSpec((tm, d), lambda i, j: (i, 0)),
        out_shape=jax.ShapeDtypeStruct((s, d), F32),
        scratch_shapes=[pltpu.VMEM((tm, d), BF16), pltpu.VMEM((tm, d), F32)],
        compiler_params=pltpu.CompilerParams(
            dimension_semantics=("parallel", "arbitrary"), vmem_limit_bytes=VMEM_LIMIT),
    )(x, gain, wg, wu, wd, fgain)


def _inproj_body(x_ref, g_ref, w_ref, o_ref, xn_ref):
    @pl.when(pl.program_id(1) == 0)
    def _():
        xn_ref[...] = _rms(x_ref[...], g_ref[...]).astype(BF16)

    o_ref[...] = _dot(xn_ref[...], w_ref[...])


def _inproj(h, gain, w, *, tm=1024, tn=512):
    s, d = h.shape
    n = w.shape[1]
    return pl.pallas_call(
        _inproj_body,
        name="inproj",
        grid=(s // tm, n // tn),
        in_specs=[
            pl.BlockSpec((tm, d), lambda i, j: (i, 0)),
            pl.BlockSpec((1, d), lambda i, j: (0, 0)),
            pl.BlockSpec((d, tn), lambda i, j: (0, j)),
        ],
        out_specs=pl.BlockSpec((tm, tn), lambda i, j: (i, j)),
        out_shape=jax.ShapeDtypeStruct((s, n), F32),
        scratch_shapes=[pltpu.VMEM((tm, d), BF16)],
        compiler_params=pltpu.CompilerParams(
            dimension_semantics=("parallel", "arbitrary"), vmem_limit_bytes=VMEM_LIMIT),
    )(h, gain, w)


def _rope_tables(s, head_dim):
    rot = head_dim // ROPE_FRACTION
    half = rot // 2
    inv_freq = ROPE_THETA ** (-(jnp.arange(half, dtype=F32) * 2.0 / rot))
    ang = jnp.arange(s, dtype=F32)[:, None] * inv_freq[None, :]
    cos, sin = jnp.cos(ang), jnp.sin(ang)
    ones = jnp.ones((s, head_dim - rot), F32)
    zeros = jnp.zeros((s, head_dim - rot), F32)
    zh = jnp.zeros((s, half), F32)
    c = jnp.concatenate([cos, cos, ones], axis=1)
    sp = jnp.concatenate([zh, sin, zeros], axis=1)
    sm = jnp.concatenate([-sin, zh, zeros], axis=1)
    reps = LANES // head_dim
    return tuple(jnp.tile(t, (1, reps)) for t in (c, sp, sm))


def _rope128(x, c, sp, sm, half):
    return x * c + pltpu.roll(x, half, 1) * sp + pltpu.roll(x, LANES - half, 1) * sm


def _attn_prep_body(q_ref, kv_ref, qi_ref, misc_ref, c1_ref, sp1_ref, sm1_ref, c2_ref, sp2_ref, sm2_ref,
                    qo_ref, ko_ref, vo_ref, qio_ref, kio_ref, wio_ref):
    c1, sp1, sm1 = c1_ref[...], sp1_ref[...], sm1_ref[...]
    c2, sp2, sm2 = c2_ref[...], sp2_ref[...], sm2_ref[...]
    att_scale = ATTN_HEAD_DIM ** -0.5 * LOG2E
    for b in range(ATTN_WIDTH // LANES):
        sl = slice(b * LANES, (b + 1) * LANES)
        qo_ref[:, sl] = (_rope128(q_ref[:, sl], c1, sp1, sm1, 16) * att_scale).astype(BF16)
    for b in range(KV_WIDTH // LANES):
        sl = slice(b * LANES, (b + 1) * LANES)
        ko_ref[:, sl] = _rope128(kv_ref[:, sl], c1, sp1, sm1, 16).astype(BF16)
    vt = kv_ref[:, KV_WIDTH:].T.astype(BF16)
    ones = jnp.ones((VT_ROWS - ATTN_HEAD_DIM, vt.shape[1]), BF16)
    for g in range(ATTN_KV_HEADS):
        vo_ref[0, g * VT_ROWS:g * VT_ROWS + ATTN_HEAD_DIM, :] = vt[g * ATTN_HEAD_DIM:(g + 1) * ATTN_HEAD_DIM]
        vo_ref[0, g * VT_ROWS + ATTN_HEAD_DIM:(g + 1) * VT_ROWS, :] = ones
    for b in range(IDX_HEADS * IDX_HEAD_DIM // LANES):
        sl = slice(b * LANES, (b + 1) * LANES)
        qio_ref[:, sl] = _rope128(qi_ref[:, sl], c2, sp2, sm2, 8).astype(BF16)
    ki = _rope128(misc_ref[:, :LANES], c2, sp2, sm2, 8)
    kio_ref[...] = ki[:, :IDX_HEAD_DIM].astype(BF16)
    idx_scale = (IDX_HEADS ** -0.5) * (IDX_HEAD_DIM ** -0.5)
    wio_ref[...] = (misc_ref[:, 2 * LANES:3 * LANES] * idx_scale).T[:IDX_HEADS]


def _attn_prep(u, tabs128, tabs64, *, tm):
    s = u.shape[0]
    tab_spec = pl.BlockSpec((tm, LANES), lambda i: (i, 0))
    return pl.pallas_call(
        _attn_prep_body,
        name="attn_prep",
        grid=(s // tm,),
        in_specs=[
            pl.BlockSpec((tm, ATTN_WIDTH), lambda i: (i, U_Q // ATTN_WIDTH)),
            pl.BlockSpec((tm, 512), lambda i: (i, U_KV // 512)),
            pl.BlockSpec((tm, 512), lambda i: (i, U_QI // 512)),
            pl.BlockSpec((tm, 512), lambda i: (i, U_MISC // 512)),
        ] + [tab_spec] * 6,
        out_specs=[
            pl.BlockSpec((tm, ATTN_WIDTH), lambda i: (i, 0)),
            pl.BlockSpec((tm, KV_WIDTH), lambda i: (i, 0)),
            pl.BlockSpec((1, ATTN_KV_HEADS * VT_ROWS, tm), lambda i: (i, 0, 0)),
            pl.BlockSpec((tm, 512), lambda i: (i, 0)),
            pl.BlockSpec((tm, IDX_HEAD_DIM), lambda i: (i, 0)),
            pl.BlockSpec((IDX_HEADS, tm), lambda i: (0, i)),
        ],
        out_shape=[
            jax.ShapeDtypeStruct((s, ATTN_WIDTH), BF16),
            jax.ShapeDtypeStruct((s, KV_WIDTH), BF16),
            jax.ShapeDtypeStruct((s // tm, ATTN_KV_HEADS * VT_ROWS, tm), BF16),
            jax.ShapeDtypeStruct((s, 512), BF16),
            jax.ShapeDtypeStruct((s, IDX_HEAD_DIM), BF16),
            jax.ShapeDtypeStruct((IDX_HEADS, s), F32),
        ],
        compiler_params=pltpu.CompilerParams(
            dimension_semantics=("parallel",), vmem_limit_bytes=VMEM_LIMIT),
    )(u, u, u, u, *tabs128, *tabs64)


def _col_reduce(x, op):
    n, w = x.shape
    y = x.reshape(n // 64, 64, w)
    acc = y[0]
    for a in range(1, n // 64):
        acc = op(acc, y[a])
    z = acc.reshape(8, 8, w)
    t = [op(z[2 * i], z[2 * i + 1]) for i in range(4)]
    r = op(op(t[0], t[1]), op(t[2], t[3]))
    red = jnp.max if op is jnp.maximum else jnp.sum
    return red(r, axis=0, keepdims=True)


def _dsa_body(q_ref, qi_ref, wt_ref, k_ref, vt_ref, ki_ref, o_ref,
              key_ref, hi_ref, lo_ref, qs_ref, qis_ref, p1_ref, p2_ref, r1_ref, r2_ref, run_ref, accr_ref,
              flag_ref, s_ref, s2_ref, smax_ref, smax2_ref, m_ref, acc_ref, *, tq, tk, k_sel, pos_span):
    qb = pl.program_id(0)
    q_end = (qb + 1) * tq
    n_kt = (q_end + tk - 1) // tk
    n_full = (qb * tq) // tk
    n_pair = ATTN_HEADS // 2

    for h in range(ATTN_HEADS):
        qs_ref[h * tq:(h + 1) * tq, :] = q_ref[:, h * ATTN_HEAD_DIM:(h + 1) * ATTN_HEAD_DIM]
    for h in range(IDX_HEADS):
        qis_ref[h * tq:(h + 1) * tq, :] = qi_ref[:, h * IDX_HEAD_DIM:(h + 1) * IDX_HEAD_DIM]

    qpos = lax.broadcasted_iota(I32, (1, tq), 1) + qb * tq
    key_lim = (qpos // CHUNK + 1) * CHUNK
    wt = wt_ref[...]

    def score_tile(jt, masked):
        kit = ki_ref[pl.ds(pl.multiple_of(jt * tk, tk), tk), :]
        acc = jnp.zeros((tk, tq), F32)
        for p in range(IDX_HEADS // 2):
            s2 = _dot_nt(kit, qis_ref[2 * p * tq:(2 * p + 2) * tq, :])
            acc = acc + jnp.maximum(s2[:, :tq], 0.0) * wt[2 * p:2 * p + 1, :]
            acc = acc + jnp.maximum(s2[:, tq:], 0.0) * wt[2 * p + 1:2 * p + 2, :]
        bits = pltpu.bitcast(acc, I32)
        key = bits ^ ((bits >> 31) & 0x7FFFFFFF)
        kpos = lax.broadcasted_iota(I32, (tk, tq), 0) + jt * tk
        key = jnp.where(key == 0, -1 - kpos, key)
        if masked:
            key = jnp.where(kpos < key_lim, key, INT_MIN)
        key_ref[jt] = key
        hi_ref[jt] = (key >> 16).astype(I16)
        lo_ref[jt] = (key ^ 0x8000).astype(I16)

    def two_full_tiles(i, c):
        score_tile(2 * i, False)
        score_tile(2 * i + 1, False)
        return c

    def diag_tile(jt, c):
        score_tile(jt, True)
        return c

    lax.fori_loop(0, n_full // 2, two_full_tiles, 0)
    lax.fori_loop(n_full - n_full % 2, n_kt, diag_tile, 0)

    @pl.when(n_kt % 2 == 1)
    def _():
        key_ref[n_kt] = jnp.full((tk, tq), INT_MIN, I32)

    one16 = jnp.ones((tk, tq), I16)
    zero16 = jnp.zeros((tk, tq), I16)

    def fold(cnt, hit):
        hit = hit.reshape(tk // CNT_ROWS, CNT_ROWS, tq)
        for a in range(tk // CNT_ROWS):
            cnt = cnt + hit[a]
        return cnt

    def total(cnt):
        return jnp.sum(cnt.astype(I32), axis=0, keepdims=True)

    def search16(half_ref, need, start=I16_MIN, bits=16):
        def bit_step(i, carry):
            base, got = carry
            cand = base + lax.shift_left(jnp.int32(1), bits - 1 - i)
            cand16 = cand.astype(I16)

            def count_tile(jt, cnt):
                return fold(cnt, jnp.where(half_ref[jt] >= cand16, one16, zero16))

            tot = total(lax.fori_loop(0, n_kt, count_tile, jnp.zeros((CNT_ROWS, tq), I16)))
            ok = tot >= need
            return jnp.where(ok, cand, base), jnp.where(ok, tot, got)

        init = (jnp.full((1, tq), start, I32), jnp.full((1, tq), UNKNOWN_COUNT, I32))
        return lax.fori_loop(0, bits, bit_step, init)

    hi_thr, _ = search16(hi_ref, k_sel)
    hi_thr16 = hi_thr.astype(I16)

    def bucket_tile(jt, cnt):
        hi = hi_ref[jt]
        lo_ref[jt] = jnp.where(hi == hi_thr16, lo_ref[jt], jnp.int16(I16_MIN))
        return fold(cnt, jnp.where(hi > hi_thr16, one16, zero16))

    above = total(lax.fori_loop(0, n_kt, bucket_tile, jnp.zeros((CNT_ROWS, tq), I16)))
    lo_thr, reach = search16(lo_ref, k_sel - above)
    thr_raw = hi_thr * 65536 + (lo_thr + 32768)
    thr = jnp.maximum(thr_raw, INT_MIN + 1)

    maybe_tie = ((above + reach) > k_sel) & (thr_raw > INT_MIN)

    @pl.when(jnp.max(jnp.where(maybe_tie, 1, 0)) > 0)
    def _():
        rows = lax.broadcasted_iota(I32, (tk, tq), 0)
        rows16 = rows.astype(I16)
        lo_thr16 = lo_thr.astype(I16)

        def earliness_tile(jt, cnt):
            lo = lo_ref[jt]
            tied = (hi_ref[jt] == hi_thr16) & (lo == lo_thr16)
            hi_ref[jt] = jnp.where(tied, (POS_TOP - jt * tk).astype(I16) - rows16, jnp.int16(I16_MIN))
            return fold(cnt, jnp.where(lo > lo_thr16, one16, zero16))

        beyond = above + total(lax.fori_loop(0, n_kt, earliness_tile, jnp.zeros((CNT_ROWS, tq), I16)))
        keep = jnp.where(maybe_tie, k_sel - beyond, UNKNOWN_COUNT)
        first_kept, _ = search16(hi_ref, keep, start=POS_TOP + 1 - pos_span, bits=pos_span.bit_length() - 1)
        last = POS_TOP - first_kept

        def retire_tile(jt, c):
            key = key_ref[jt]
            drop = (key == thr_raw) & (rows + jt * tk > last) & maybe_tie
            key_ref[jt] = jnp.where(drop, INT_MIN, key)
            return c

        lax.fori_loop(0, n_kt, retire_tile, 0)

    def masked_scores(jt, p, sel2):
        g = p // (n_pair // ATTN_KV_HEADS)
        off = pl.multiple_of(jt * tk, tk)
        kg = k_ref[pl.ds(off, tk), g * ATTN_HEAD_DIM:(g + 1) * ATTN_HEAD_DIM]
        s = _dot_nt(kg, qs_ref[2 * p * tq:(2 * p + 2) * tq, :])
        return s if sel2 is None else jnp.where(sel2, s, NEG_BIG)

    def selection(jt):
        sel = key_ref[jt] >= thr
        return jnp.concatenate([sel, sel], axis=1)

    def values(jt, p):
        g = p // (n_pair // ATTN_KV_HEADS)
        return vt_ref[jt, g * VT_ROWS:(g + 1) * VT_ROWS, :]

    def two_tiles_loop(stage, fold_in, bufs):
        stage(bufs[0], 0)

        def trip(i, c):
            j0 = 2 * i
            stage(bufs[1], j0 + 1)
            fold_in(bufs[0], j0)
            stage(bufs[0], jnp.minimum(j0 + 2, n_kt - 1))
            fold_in(bufs[1], j0 + 1)
            return c

        lax.fori_loop(0, (n_kt + 1) // 2, trip, 0)

    for p in range(n_pair):
        r0 = _col_reduce(masked_scores(0, p, None), jnp.maximum)
        run_ref[p] = r0
        accr_ref[p] = r0
    flag_ref[...] = jnp.full(flag_ref.shape, NEG_BIG, F32)
    acc_ref[...] = jnp.zeros(acc_ref.shape, F32)

    def stage_probs(buf, jt):
        p_ref, r_ref = buf
        sel2 = selection(jt)
        for p in range(n_pair):
            s = masked_scores(jt, p, sel2)
            r = run_ref[p]
            part = None
            for c0 in range(0, tk, 64):
                sc = s[c0:c0 + 64]
                p_ref[p, c0:c0 + 64, :] = jnp.exp2(sc - r).astype(BF16)
                part = sc if part is None else jnp.maximum(part, sc)
            tmax = _col_reduce(part, jnp.maximum)
            r_ref[p] = r
            flag_ref[p] = jnp.maximum(flag_ref[p], tmax - r)
            run_ref[p] = jnp.maximum(r, tmax)

    def fold_probs(buf, jt):
        p_ref, r_ref = buf
        for p in range(n_pair):
            r = r_ref[p]
            acc_ref[p] = acc_ref[p] * jnp.exp2(accr_ref[p] - r) + _dot(values(jt, p), p_ref[p])
            accr_ref[p] = r

    two_tiles_loop(stage_probs, fold_probs, ((p1_ref, r1_ref), (p2_ref, r2_ref)))

    bad = jnp.float32(0.0)
    for p in range(n_pair):
        den = acc_ref[p][ATTN_HEAD_DIM:ATTN_HEAD_DIM + 1]
        ok = (flag_ref[p] < EXP2_HEADROOM) & (den > DEN_MIN) & (den < DEN_MAX)
        bad = jnp.maximum(bad, jnp.max(jnp.where(ok, 0.0, 1.0)))

    @pl.when(bad > 0.0)
    def _():
        m_ref[...] = jnp.full(m_ref.shape, NEG_BIG, F32)
        acc_ref[...] = jnp.zeros(acc_ref.shape, F32)

        def stage_scores(buf, jt):
            s_buf, max_buf = buf
            sel2 = selection(jt)
            for p in range(n_pair):
                s = masked_scores(jt, p, sel2)
                s_buf[p] = s
                max_buf[p] = _col_reduce(s, jnp.maximum)

        def fold_scores(buf, jt):
            s_buf, max_buf = buf
            for p in range(n_pair):
                m_prev = m_ref[p]
                m_cur = jnp.maximum(m_prev, max_buf[p])
                pr = jnp.exp2(s_buf[p] - m_cur).astype(BF16)
                acc_ref[p] = acc_ref[p] * jnp.exp2(m_prev - m_cur) + _dot(values(jt, p), pr)
                m_ref[p] = m_cur

        two_tiles_loop(stage_scores, fold_scores, ((s_ref, smax_ref), (s2_ref, smax2_ref)))

    for p in range(n_pair):
        acc = acc_ref[p]
        o = acc[:ATTN_HEAD_DIM] / acc[ATTN_HEAD_DIM:ATTN_HEAD_DIM + 1]
        for r in range(2):
            h = 2 * p + r
            o_ref[:, h * ATTN_HEAD_DIM:(h + 1) * ATTN_HEAD_DIM] = o[:, r * tq:(r + 1) * tq].T.astype(BF16)


def _dsa(q, qi, wt, k, vt, ki, *, tq=128):
    s = q.shape[0]
    tk = vt.shape[2]
    k_sel = min(TOPK_MAX, s // 4)
    assert s <= POS_TOP + 1 and (s // tk) % 2 == 0 and s % tq == 0
    whole = lambda shape: pl.BlockSpec(shape, lambda i: (0,) * len(shape), pipeline_mode=pl.Buffered(1))
    return pl.pallas_call(
        functools.partial(_dsa_body, tq=tq, tk=tk, k_sel=k_sel, pos_span=1 << (s - 1).bit_length()),
        name="dsa",
        grid=(s // tq,),
        in_specs=[
            pl.BlockSpec((tq, ATTN_WIDTH), lambda i: (i, 0)),
            pl.BlockSpec((tq, 512), lambda i: (i, 0)),
            pl.BlockSpec((IDX_HEADS, tq), lambda i: (0, i)),
            whole((s, KV_WIDTH)),
            whole((s // tk, ATTN_KV_HEADS * VT_ROWS, tk)),
            whole((s, IDX_HEAD_DIM)),
        ],
        out_specs=pl.BlockSpec((tq, ATTN_WIDTH), lambda i: (i, 0)),
        out_shape=jax.ShapeDtypeStruct((s, ATTN_WIDTH), BF16),
        scratch_shapes=[
            pltpu.VMEM((s // tk, tk, tq), I32),
            pltpu.VMEM((s // tk, tk, tq), I16),
            pltpu.VMEM((s // tk, tk, tq), I16),
            pltpu.VMEM((ATTN_HEADS * tq, ATTN_HEAD_DIM), BF16),
            pltpu.VMEM((IDX_HEADS * tq, IDX_HEAD_DIM), BF16),
            pltpu.VMEM((ATTN_HEADS // 2, tk, 2 * tq), BF16),
            pltpu.VMEM((ATTN_HEADS // 2, tk, 2 * tq), BF16),
            pltpu.VMEM((ATTN_HEADS // 2, 1, 2 * tq), F32),
            pltpu.VMEM((ATTN_HEADS // 2, 1, 2 * tq), F32),
            pltpu.VMEM((ATTN_HEADS // 2, 1, 2 * tq), F32),
            pltpu.VMEM((ATTN_HEADS // 2, 1, 2 * tq), F32),
            pltpu.VMEM((ATTN_HEADS // 2, 1, 2 * tq), F32),
            pltpu.VMEM((ATTN_HEADS // 2, tk, 2 * tq), F32),
            pltpu.VMEM((ATTN_HEADS // 2, tk, 2 * tq), F32),
            pltpu.VMEM((ATTN_HEADS // 2, 1, 2 * tq), F32),
            pltpu.VMEM((ATTN_HEADS // 2, 1, 2 * tq), F32),
            pltpu.VMEM((ATTN_HEADS // 2, 1, 2 * tq), F32),
            pltpu.VMEM((ATTN_HEADS // 2, VT_ROWS, 2 * tq), F32),
        ],
        compiler_params=pltpu.CompilerParams(
            dimension_semantics=("parallel",), vmem_limit_bytes=VMEM_LIMIT),
    )(q, qi, wt, k, vt, ki)


def _token_shift(x, prev8, first):
    rolled = pltpu.roll(x, 1, 0)
    prev_row = jnp.where(first, 0.0, prev8[7:8, :])
    row = lax.broadcasted_iota(I32, x.shape, 0)
    return jnp.where(row == 0, jnp.broadcast_to(prev_row, x.shape), rolled)


def _rwkv_prep_body(rr_ref, rk_ref, rv_ref, ms_ref, prr_ref, prk_ref, prv_ref, pms_ref,
                    mur_ref, muk_ref, muv_ref, mum_ref, w0_ref, w2_ref, a0_ref, a2_ref, g2_ref,
                    kk_ref, ka_ref, rkk_ref, bd_ref, tri_ref, blk_ref,
                    mc_ref, nc_ref, yc_ref, y0_ref, bonus_ref, g_ref,
                    at_s, rt_s, bt_s, kt_s, v_s, bdt_s, kdt_s, pct_s):
    first = pl.program_id(0) == 0
    tm = rr_ref.shape[0]

    def lerp(x_ref, p_ref, mu_ref):
        x = x_ref[...]
        return x + (_token_shift(x, p_ref[...], first) - x) * mu_ref[...]

    r = lerp(rr_ref, prr_ref, mur_ref)
    k = lerp(rk_ref, prk_ref, muk_ref)
    v = lerp(rv_ref, prv_ref, muv_ref)
    ms = lerp(ms_ref, pms_ref, mum_ref)
    xw = ms[:, :LANES]
    xag = ms[:, LANES:]

    w = w0_ref[...] + _dot(jnp.tanh(xw).astype(BF16), w2_ref[...])
    z = -w
    softplus = jnp.maximum(z, 0.0) + jnp.log1p(jnp.exp(-jnp.abs(z)))
    lw = -jnp.exp(-softplus - 0.5)
    a_lr = jax.nn.sigmoid(a0_ref[...] + _dot(xag.astype(BF16), a2_ref[...]))
    g_ref[...] = _dot(jax.nn.sigmoid(xag).astype(BF16), g2_ref[...])

    bd = bd_ref[...]
    kk = k * kk_ref[...]
    nrm = jnp.sqrt(_dot_seg(kk * kk, bd))
    kk = kk / jnp.maximum(nrm, 1e-12)
    k = k * (1.0 + (a_lr - 1.0) * ka_ref[...])
    bonus_ref[...] = _dot_seg(r * k * rkk_ref[...], bd) * v

    cum = _dot_exact_lhs(tri_ref[...], lw)
    tot = _dot_exact_lhs(blk_ref[...], lw)
    e_neg = jnp.exp(-cum)
    e_dec = jnp.exp(tot - cum)
    b = kk * a_lr
    gw = RW_GROUP_LANES
    groups = RWKV_WIDTH // gw
    time_major = ((at_s, -kk * jnp.exp(cum - lw)), (rt_s, r * jnp.exp(cum)), (bt_s, b * e_neg),
                  (kt_s, k * e_neg), (v_s, v))
    chan_major = ((bdt_s, (b * e_dec).T), (kdt_s, (k * e_dec).T), (pct_s, jnp.exp(tot).T))
    for g in range(groups):
        for ref, val in time_major:
            ref[g] = val[:, g * gw:(g + 1) * gw]
        for ref, val in chan_major:
            ref[g] = val[g * gw:(g + 1) * gw, :]

    def group_maps(g, c):
        _chunk_maps(at_s.at[g], rt_s.at[g], bt_s.at[g], kt_s.at[g], v_s.at[g],
                    bdt_s.at[g], kdt_s.at[g], pct_s.at[g],
                    mc_ref.at[g], nc_ref.at[g], yc_ref.at[g], y0_ref.at[g],
                    heads=gw // RWKV_HEAD_DIM, chunks=tm // RW_CHUNK)
        return c

    lax.fori_loop(0, groups, group_maps, 0)


def _rwkv_prep(u, p, *, tm=512):
    s = u.shape[0]
    w = RWKV_WIDTH
    t8 = tm // 8
    tile = lambda col: pl.BlockSpec((tm, w), lambda i: (i, col // w))
    prev = lambda col: pl.BlockSpec((8, w), lambda i: (jnp.maximum(i * t8 - 1, 0), col // w))
    vec = lambda n: pl.BlockSpec((1, n), lambda i: (0, 0))
    mat = lambda a, b: pl.BlockSpec((a, b), lambda i: (0, 0))
    tm_spec = pl.BlockSpec((tm, w), lambda i: (i, 0))
    gw = RW_GROUP_LANES
    groups = w // gw
    map_spec = pl.BlockSpec((groups, tm, gw), lambda i: (0, i, 0))
    outs = pl.pallas_call(
        _rwkv_prep_body,
        name="rwkv_prep",
        grid=(s // tm,),
        in_specs=[
            tile(U_RR), tile(U_RK), tile(U_RV),
            pl.BlockSpec((tm, 256), lambda i: (i, U_MISC // 256)),
            prev(U_RR), prev(U_RK), prev(U_RV),
            pl.BlockSpec((8, 256), lambda i: (jnp.maximum(i * t8 - 1, 0), U_MISC // 256)),
            vec(w), vec(w), vec(w), vec(256),
            vec(w), mat(LANES, w), vec(w), mat(LANES, w), mat(LANES, w),
            vec(w), vec(w), vec(w),
            mat(w, w), mat(tm, tm), mat(tm, tm),
        ],
        out_specs=[map_spec] * 4 + [tm_spec] * 2,
        out_shape=[jax.ShapeDtypeStruct((groups, s, gw), F32)] * 4 + [jax.ShapeDtypeStruct((s, w), F32)] * 2,
        scratch_shapes=[pltpu.VMEM((groups, tm, gw), F32)] * 5 + [pltpu.VMEM((groups, gw, tm), F32)] * 3,
        compiler_params=pltpu.CompilerParams(
            dimension_semantics=("parallel",), vmem_limit_bytes=VMEM_LIMIT),
    )(u, u, u, u, u, u, u, u,
      p["mu_r"], p["mu_k"], p["mu_v"], p["mu_m"], p["w0"], p["w2"], p["a0"], p["a2"], p["g2"],
      p["k_k"], p["k_a"], p["r_k"], p["bd"], p["tri"], p["blk"])
    return outs


def _mm1(a, b):
    return _dot(a.astype(BF16), b.astype(BF16))


def _chunk_maps(at_ref, rt_ref, bt_ref, kt_ref, v_ref, bdt_ref, kdt_ref, pct_ref,
                mc_ref, nc_ref, yc_ref, y0_ref, *, heads, chunks):
    n = RWKV_HEAD_DIM
    cl = RW_CHUNK
    ri = lax.broadcasted_iota(I32, (cl, cl), 0)
    ci = lax.broadcasted_iota(I32, (cl, cl), 1)
    strict = ri > ci
    incl = ri >= ci
    same16 = (ri // 16) == (ci // 16)
    eye = (ri == ci).astype(F32)

    pairs = [(slice(cc * cl, (cc + 1) * cl), slice(hh * n, (hh + 1) * n))
             for hh in range(heads) for cc in range(chunks)]
    each = lambda fn, *lists: [fn(*xs) for xs in zip(*lists)]

    at = [at_ref[ts, ls] for ts, ls in pairs]
    rt = [rt_ref[ts, ls] for ts, ls in pairs]
    v = [v_ref[ts, ls] for ts, ls in pairs]
    g = [_mm3_nt(jnp.concatenate([a, r], axis=0), jnp.concatenate([bt_ref[ts, ls], kt_ref[ts, ls]], axis=0))
         for a, r, (ts, ls) in zip(at, rt, pairs)]
    l = [jnp.where(strict, x[:cl, :cl], 0.0) for x in g]
    a_ak = [jnp.where(strict, x[:cl, cl:], 0.0) for x in g]
    a_rb = [jnp.where(incl, x[cl:, :cl], 0.0) for x in g]
    a_rk = [jnp.where(incl, x[cl:, cl:], 0.0) for x in g]

    ld = [jnp.where(same16, x, 0.0) for x in l]
    lo = each(lambda x, y: x - y, l, ld)
    d = [eye + x for x in ld]
    pw = ld
    for _ in range(3):
        pw = each(_mm1, pw, pw)
        d = each(lambda x, y: x + _mm1(x, y), d, pw)
    m = each(_mm1, d, lo)
    m2 = each(_mm1, m, m)
    f = each(lambda x, y: x + _mm1(y, x), d, m2)
    t = each(lambda x, y: x + _mm1(y, x), f, m)

    w1 = each(_mm1, a_ak, v)
    x = each(lambda tt, a, w: _mm1(tt, jnp.concatenate([a, w], axis=1)), t, at, w1)
    big = [_mm1(jnp.concatenate([bdt_ref[ls, ts], arb], axis=0), xx)
           for arb, xx, (ts, ls) in zip(a_rb, x, pairs)]
    kv = [_mm1(jnp.concatenate([kdt_ref[ls, ts], ark], axis=0), vv)
          for ark, vv, (ts, ls) in zip(a_rk, v, pairs)]
    for i, (ts, ls) in enumerate(pairs):
        mc_ref[ts, ls] = eye * pct_ref[ls, ts] + big[i][:n, :n]
        nc_ref[ts, ls] = big[i][:n, n:] + kv[i][:n]
        yc_ref[ts, ls] = rt[i] + big[i][n:, :n]
        y0_ref[ts, ls] = big[i][n:, n:] + kv[i][n:]


def _rwkv_state_body(mc_ref, nc_ref, yc_ref, y0_ref, y_ref, st_ref, *, chunks):
    n = RWKV_HEAD_DIM
    cl = RW_CHUNK
    per_group = RW_GROUP_LANES // n

    @pl.when(pl.program_id(0) == 0)
    def _():
        st_ref[...] = jnp.zeros_like(st_ref)

    heads = [(hh // per_group, slice((hh % per_group) * n, (hh % per_group + 1) * n), slice(hh * n, (hh + 1) * n))
             for hh in range(RWKV_HEADS)]
    st = [st_ref[:, ls] for _, _, ls in heads]
    for cc in range(chunks):
        ts = slice(cc * cl, (cc + 1) * cl)
        r = [_mm3(jnp.concatenate([mc_ref[g, ts, gl], yc_ref[g, ts, gl]], axis=0), s)
             for (g, gl, _), s in zip(heads, st)]
        for (g, gl, ls), x in zip(heads, r):
            y_ref[ts, ls] = x[n:] + y0_ref[g, ts, gl]
        st = [x[:n] + nc_ref[g, ts, gl] for (g, gl, _), x in zip(heads, r)]
    for (_, _, ls), s in zip(heads, st):
        st_ref[:, ls] = s


def _rwkv_state(mc, nc, yc, y0, *, chunks=4):
    groups, s, gw = mc.shape
    w = groups * gw
    rows = chunks * RW_CHUNK
    blk = pl.BlockSpec((groups, rows, gw), lambda c: (0, c, 0))
    return pl.pallas_call(
        functools.partial(_rwkv_state_body, chunks=chunks),
        name="rwkv_state",
        grid=(s // rows,),
        in_specs=[blk] * 4,
        out_specs=pl.BlockSpec((rows, w), lambda c: (c, 0)),
        out_shape=jax.ShapeDtypeStruct((s, w), F32),
        scratch_shapes=[pltpu.VMEM((RWKV_HEAD_DIM, w), F32)],
        compiler_params=pltpu.CompilerParams(
            dimension_semantics=("arbitrary",), vmem_limit_bytes=VMEM_LIMIT),
    )(mc, nc, yc, y0)


def _outproj_body(h_ref, attn_ref, y_ref, bonus_ref, g_ref, gain_ref, bias_ref, bd_ref, wa_ref, wr_ref, o_ref):
    y = y_ref[...]
    bd = bd_ref[...]
    inv_n = 1.0 / RWKV_HEAD_DIM
    mean = _dot_seg(y, bd) * inv_n
    yc = y - mean
    var = _dot_seg(yc * yc, bd) * inv_n
    yn = yc * lax.rsqrt(var + GN_EPS) * gain_ref[...] + bias_ref[...]
    rw = ((yn + bonus_ref[...]) * g_ref[...]).astype(BF16)
    o_ref[...] = h_ref[...] + _dot(attn_ref[...], wa_ref[...]) + _dot(rw, wr_ref[...])


def _outproj(h, attn, y, bonus, g, gain, bias, bd, wa, wr, *, tm=256):
    s, d = h.shape
    w = RWKV_WIDTH
    row = lambda n: pl.BlockSpec((tm, n), lambda i: (i, 0))
    whole = lambda a, b: pl.BlockSpec((a, b), lambda i: (0, 0))
    return pl.pallas_call(
        _outproj_body,
        name="outproj",
        grid=(s // tm,),
        in_specs=[row(d), row(w), row(w), row(w), row(w), whole(1, w), whole(1, w), whole(w, w),
                  whole(ATTN_WIDTH, d), whole(w, d)],
        out_specs=row(d),
        out_shape=jax.ShapeDtypeStruct((s, d), F32),
        compiler_params=pltpu.CompilerParams(
            dimension_semantics=("parallel",), vmem_limit_bytes=VMEM_LIMIT),
    )(h, attn, y, bonus, g, gain, bias, bd, wa, wr)


def _relayout_w_in(w_in):
    d = w_in.shape[0]
    o = np.cumsum([0, 1024, 256, 256, 512, 64, 8, 1024, 1024, 1024, 64, 64, 64])
    seg = lambda i: w_in[:, o[i]:o[i + 1]]
    q, k, v, qi, ki, wi, rr, rk, rv, wd, ad, gd = (seg(i) for i in range(12))
    z = lambda n: jnp.zeros((d, n), w_in.dtype)
    cols = [q, k, v, qi, rr, rk, rv, ki, wd, ad, gd, wi, z(LANES - 8), z(LANES)]
    return jnp.concatenate(cols, axis=1).astype(BF16)


def _pad_rows(w2, lo):
    z = jnp.zeros_like(w2)
    return (jnp.concatenate([w2, z], axis=0) if lo == 0 else jnp.concatenate([z, w2], axis=0)).astype(BF16)


def kernel(x, ffn1_norm, ffn1_w_gate, ffn1_w_up, ffn1_w_down, mix_norm, w_in, w_out, rwkv_mu_r, rwkv_mu_k, rwkv_mu_v, rwkv_mu_w, rwkv_mu_a, rwkv_mu_g, rwkv_w0, rwkv_w2, rwkv_a0, rwkv_a2, rwkv_g2, rwkv_k_k, rwkv_k_a, rwkv_r_k, rwkv_gn_gain, rwkv_gn_bias, ffn2_norm, ffn2_w_gate, ffn2_w_up, ffn2_w_down, final_norm):
    b, s, d = x.shape
    assert b == 1 and d == D_MODEL and ffn1_norm.shape[0] == 1
    h = x[0]
    row = lambda a: a.reshape(1, -1).astype(F32)
    z64 = jnp.zeros((1, LORA), F32)

    tm_prep = 256
    tix = np.arange(tm_prep)
    same_chunk = (tix[:, None] // RW_CHUNK) == (tix[None, :] // RW_CHUNK)
    hix = np.arange(RWKV_WIDTH) // RWKV_HEAD_DIM
    prep = dict(
        mu_r=row(rwkv_mu_r[0]), mu_k=row(rwkv_mu_k[0]), mu_v=row(rwkv_mu_v[0]),
        mu_m=jnp.concatenate([z64, row(rwkv_mu_w[0]), row(rwkv_mu_a[0]), row(rwkv_mu_g[0])], axis=1),
        w0=row(rwkv_w0[0]), w2=_pad_rows(rwkv_w2[0], LORA),
        a0=row(rwkv_a0[0]), a2=_pad_rows(rwkv_a2[0], 0), g2=_pad_rows(rwkv_g2[0], LORA),
        k_k=row(rwkv_k_k[0]), k_a=row(rwkv_k_a[0]), r_k=row(rwkv_r_k[0]),
        bd=jnp.asarray(hix[:, None] == hix[None, :], BF16),
        tri=jnp.asarray(same_chunk & (tix[:, None] >= tix[None, :]), BF16),
        blk=jnp.asarray(same_chunk, BF16),
    )

    h1 = _ffn(h, row(ffn1_norm[0]), ffn1_w_gate[0].astype(BF16), ffn1_w_up[0].astype(BF16),
              ffn1_w_down[0].astype(BF16), row(final_norm), final_norm=False)
    u = _inproj(h1, row(mix_norm[0]), _relayout_w_in(w_in[0]))

    q, k, vt, qi, ki, wt = _attn_prep(u, _rope_tables(s, ATTN_HEAD_DIM), _rope_tables(s, IDX_HEAD_DIM),
                                     tm=DSA_KEY_TILE)
    attn = _dsa(q, qi, wt, k, vt, ki)

    mc, nc, yc, y0, bonus, g = _rwkv_prep(u, prep, tm=tm_prep)
    y = _rwkv_state(mc, nc, yc, y0)

    wo = w_out[0].astype(BF16)
    h2 = _outproj(h1, attn, y, bonus, g, row(rwkv_gn_gain[0]), row(rwkv_gn_bias[0]), prep["bd"],
                  wo[:ATTN_WIDTH], wo[ATTN_WIDTH:])
    out = _ffn(h2, row(ffn2_norm[0]), ffn2_w_gate[0].astype(BF16), ffn2_w_up[0].astype(BF16),
               ffn2_w_down[0].astype(BF16), row(final_norm), final_norm=True)
    return out[None]
```

```python
import functools

import jax
import jax.numpy as jnp
import numpy as np
from jax import lax
from jax.experimental import pallas as pl
from jax.experimental.pallas import tpu as pltpu

F32 = jnp.float32
BF16 = jnp.bfloat16
I32 = jnp.int32

D_MODEL = 2048
CHUNK = 64
ROPE_THETA = 500000.0
ROPE_FRACTION = 4
NORM_EPS = 1e-6
ATTN_HEAD_DIM = 128
ATTN_WIDTH = 1024
ATTN_HEADS = 8
ATTN_KV_HEADS = 2
KV_WIDTH = 256
IDX_HEADS = 8
IDX_HEAD_DIM = 64
TOPK_MAX = 256
RWKV_HEAD_DIM = 64
RWKV_WIDTH = 1024
RWKV_HEADS = 16
LORA = 64
GN_EPS = 64e-5
D_FF = 5632

LANES = 128
VMEM_LIMIT = 56 * 1024 * 1024

U_WIDTH = 5632
U_Q = 0
U_KV = 1024
U_QI = 1536
U_RR = 2048
U_RK = 3072
U_RV = 4096
U_MISC = 5120

RW_CHUNK = 64
RW_GROUP_LANES = 512
DSA_KEY_TILE = 512
VT_ROWS = ATTN_HEAD_DIM + 16
EXP2_HEADROOM = 60.0
DEN_MIN, DEN_MAX = 1e-30, 1e30
CNT_ROWS = 128
I16 = jnp.int16
I16_MIN = -2 ** 15
POS_TOP = 2 ** 15 - 1
UNKNOWN_COUNT = 2 ** 30
LOG2E = 1.4426950408889634
INT_MIN = -2 ** 31
NEG_BIG = -1e30


def _rms(x, g):
    return x * lax.rsqrt(jnp.mean(x * x, axis=-1, keepdims=True) + NORM_EPS) * g


def _dot(a, b):
    return jnp.dot(a, b, preferred_element_type=F32)


def _dot_nt(a, b):
    return lax.dot_general(a, b, (((1,), (1,)), ((), ())), preferred_element_type=F32)


def _split3(x):
    hi = x.astype(BF16)
    r1 = x - hi.astype(F32)
    mid = r1.astype(BF16)
    lo = (r1 - mid.astype(F32)).astype(BF16)
    return hi, mid, lo


def _dot_seg(x, m):
    hi, lo = _split2(x)
    return _dot(hi, m) + _dot(lo, m)


def _dot_exact_lhs(m, x):
    hi, mid, lo = _split3(x)
    return _dot(m, hi) + _dot(m, mid) + _dot(m, lo)


def _split2(x):
    hi = x.astype(BF16)
    lo = (x - hi.astype(F32)).astype(BF16)
    return hi, lo


def _mm3(a, b):
    ah, al = _split2(a)
    bh, bl = _split2(b)
    return _dot(ah, bh) + _dot(ah, bl) + _dot(al, bh)


def _mm3_nt(a, b):
    ah, al = _split2(a)
    bh, bl = _split2(b)
    return _dot_nt(ah, bh) + _dot_nt(ah, bl) + _dot_nt(al, bh)


def _ffn_body(x_ref, g_ref, wg_ref, wu_ref, wd_ref, fg_ref, o_ref, xn_ref, acc_ref, *, final_norm):
    j = pl.program_id(1)

    @pl.when(j == 0)
    def _():
        xn_ref[...] = _rms(x_ref[...], g_ref[...]).astype(BF16)
        acc_ref[...] = jnp.zeros_like(acc_ref)

    xn = xn_ref[...]
    gt = _dot(xn, wg_ref[...])
    ut = _dot(xn, wu_ref[...])
    act = (gt * jax.nn.sigmoid(gt) * ut).astype(BF16)
    acc_ref[...] += _dot(act, wd_ref[...])

    @pl.when(j == pl.num_programs(1) - 1)
    def _():
        h = x_ref[...] + 0.5 * acc_ref[...]
        if final_norm:
            h = _rms(h, fg_ref[...])
        o_ref[...] = h


def _ffn(x, gain, wg, wu, wd, fgain, *, final_norm, tm=512, tf=512):
    s, d = x.shape
    f = wg.shape[1]
    return pl.pallas_call(
        functools.partial(_ffn_body, final_norm=final_norm),
        name="ffn_final" if final_norm else "ffn",
        grid=(s // tm, f // tf),
        in_specs=[
            pl.BlockSpec((tm, d), lambda i, j: (i, 0)),
            pl.BlockSpec((1, d), lambda i, j: (0, 0)),
            pl.BlockSpec((d, tf), lambda i, j: (0, j)),
            pl.BlockSpec((d, tf), lambda i, j: (0, j)),
            pl.BlockSpec((tf, d), lambda i, j: (j, 0)),
            pl.BlockSpec((1, d), lambda i, j: (0, 0)),
        ],
        out_specs=pl.BlockSpec((tm, d), lambda i, j: (i, 0)),
        out_shape=jax.ShapeDtypeStruct((s, d), F32),
        scratch_shapes=[pltpu.VMEM((tm, d), BF16), pltpu.VMEM((tm, d), F32)],
        compiler_params=pltpu.CompilerParams(
            dimension_semantics=("parallel", "arbitrary"), vmem_limit_bytes=VMEM_LIMIT),
    )(x, gain, wg, wu, wd, fgain)


def _inproj_body(x_ref, g_ref, w_ref, o_ref, xn_ref):
    @pl.when(pl.program_id(1) == 0)
    def _():
        xn_ref[...] = _rms(x_ref[...], g_ref[...]).astype(BF16)

    o_ref[...] = _dot(xn_ref[...], w_ref[...])


def _inproj(h, gain, w, *, tm=1024, tn=512):
    s, d = h.shape
    n = w.shape[1]
    return pl.pallas_call(
        _inproj_body,
        name="inproj",
        grid=(s // tm, n // tn),
        in_specs=[
            pl.BlockSpec((tm, d), lambda i, j: (i, 0)),
            pl.BlockSpec((1, d), lambda i, j: (0, 0)),
            pl.BlockSpec((d, tn), lambda i, j: (0, j)),
        ],
        out_specs=pl.BlockSpec((tm, tn), lambda i, j: (i, j)),
        out_shape=jax.ShapeDtypeStruct((s, n), F32),
        scratch_shapes=[pltpu.VMEM((tm, d), BF16)],
        compiler_params=pltpu.CompilerParams(
            dimension_semantics=("parallel", "arbitrary"), vmem_limit_bytes=VMEM_LIMIT),
    )(h, gain, w)


def _rope_tables(s, head_dim):
    rot = head_dim // ROPE_FRACTION
    half = rot // 2
    inv_freq = ROPE_THETA ** (-(jnp.arange(half, dtype=F32) * 2.0 / rot))
    ang = jnp.arange(s, dtype=F32)[:, None] * inv_freq[None, :]
    cos, sin = jnp.cos(ang), jnp.sin(ang)
    ones = jnp.ones((s, head_dim - rot), F32)
    zeros = jnp.zeros((s, head_dim - rot), F32)
    zh = jnp.zeros((s, half), F32)
    c = jnp.concatenate([cos, cos, ones], axis=1)
    sp = jnp.concatenate([zh, sin, zeros], axis=1)
    sm = jnp.concatenate([-sin, zh, zeros], axis=1)
    reps = LANES // head_dim
    return tuple(jnp.tile(t, (1, reps)) for t in (c, sp, sm))


def _rope128(x, c, sp, sm, half):
    return x * c + pltpu.roll(x, half, 1) * sp + pltpu.roll(x, LANES - half, 1) * sm


def _attn_prep_body(q_ref, kv_ref, qi_ref, misc_ref, c1_ref, sp1_ref, sm1_ref, c2_ref, sp2_ref, sm2_ref,
                    qo_ref, ko_ref, vo_ref, qio_ref, kio_ref, wio_ref):
    c1, sp1, sm1 = c1_ref[...], sp1_ref[...], sm1_ref[...]
    c2, sp2, sm2 = c2_ref[...], sp2_ref[...], sm2_ref[...]
    att_scale = ATTN_HEAD_DIM ** -0.5 * LOG2E
    for b in range(ATTN_WIDTH // LANES):
        sl = slice(b * LANES, (b + 1) * LANES)
        qo_ref[:, sl] = (_rope128(q_ref[:, sl], c1, sp1, sm1, 16) * att_scale).astype(BF16)
    for b in range(KV_WIDTH // LANES):
        sl = slice(b * LANES, (b + 1) * LANES)
        ko_ref[:, sl] = _rope128(kv_ref[:, sl], c1, sp1, sm1, 16).astype(BF16)
    vt = kv_ref[:, KV_WIDTH:].T.astype(BF16)
    ones = jnp.ones((VT_ROWS - ATTN_HEAD_DIM, vt.shape[1]), BF16)
    for g in range(ATTN_KV_HEADS):
        vo_ref[0, g * VT_ROWS:g * VT_ROWS + ATTN_HEAD_DIM, :] = vt[g * ATTN_HEAD_DIM:(g + 1) * ATTN_HEAD_DIM]
        vo_ref[0, g * VT_ROWS + ATTN_HEAD_DIM:(g + 1) * VT_ROWS, :] = ones
    for b in range(IDX_HEADS * IDX_HEAD_DIM // LANES):
        sl = slice(b * LANES, (b + 1) * LANES)
        qio_ref[:, sl] = _rope128(qi_ref[:, sl], c2, sp2, sm2, 8).astype(BF16)
    ki = _rope128(misc_ref[:, :LANES], c2, sp2, sm2, 8)
    kio_ref[...] = ki[:, :IDX_HEAD_DIM].astype(BF16)
    idx_scale = (IDX_HEADS ** -0.5) * (IDX_HEAD_DIM ** -0.5)
    wio_ref[...] = (misc_ref[:, 2 * LANES:3 * LANES] * idx_scale).T[:IDX_HEADS]


def _attn_prep(u, tabs128, tabs64, *, tm):
    s = u.shape[0]
    tab_spec = pl.BlockSpec((tm, LANES), lambda i: (i, 0))
    return pl.pallas_call(
        _attn_prep_body,
        name="attn_prep",
        grid=(s // tm,),
        in_specs=[
            pl.BlockSpec((tm, ATTN_WIDTH), lambda i: (i, U_Q // ATTN_WIDTH)),
            pl.BlockSpec((tm, 512), lambda i: (i, U_KV // 512)),
            pl.BlockSpec((tm, 512), lambda i: (i, U_QI // 512)),
            pl.BlockSpec((tm, 512), lambda i: (i, U_MISC // 512)),
        ] + [tab_spec] * 6,
        out_specs=[
            pl.BlockSpec((tm, ATTN_WIDTH), lambda i: (i, 0)),
            pl.BlockSpec((tm, KV_WIDTH), lambda i: (i, 0)),
            pl.BlockSpec((1, ATTN_KV_HEADS * VT_ROWS, tm), lambda i: (i, 0, 0)),
            pl.BlockSpec((tm, 512), lambda i: (i, 0)),
            pl.BlockSpec((tm, IDX_HEAD_DIM), lambda i: (i, 0)),
            pl.BlockSpec((IDX_HEADS, tm), lambda i: (0, i)),
        ],
        out_shape=[
            jax.ShapeDtypeStruct((s, ATTN_WIDTH), BF16),
            jax.ShapeDtypeStruct((s, KV_WIDTH), BF16),
            jax.ShapeDtypeStruct((s // tm, ATTN_KV_HEADS * VT_ROWS, tm), BF16),
            jax.ShapeDtypeStruct((s, 512), BF16),
            jax.ShapeDtypeStruct((s, IDX_HEAD_DIM), BF16),
            jax.ShapeDtypeStruct((IDX_HEADS, s), F32),
        ],
        compiler_params=pltpu.CompilerParams(
            dimension_semantics=("parallel",), vmem_limit_bytes=VMEM_LIMIT),
    )(u, u, u, u, *tabs128, *tabs64)


def _col_reduce(x, op):
    n, w = x.shape
    y = x.reshape(n // 64, 64, w)
    acc = y[0]
    for a in range(1, n // 64):
        acc = op(acc, y[a])
    z = acc.reshape(8, 8, w)
    t = [op(z[2 * i], z[2 * i + 1]) for i in range(4)]
    r = op(op(t[0], t[1]), op(t[2], t[3]))
    red = jnp.max if op is jnp.maximum else jnp.sum
    return red(r, axis=0, keepdims=True)


def _dsa_body(q_ref, qi_ref, wt_ref, k_ref, vt_ref, ki_ref, o_ref,
              key_ref, hi_ref, lo_ref, qs_ref, qis_ref, p1_ref, p2_ref, r1_ref, r2_ref, run_ref, accr_ref,
              flag_ref, s_ref, s2_ref, smax_ref, smax2_ref, m_ref, acc_ref, *, tq, tk, k_sel, pos_span):
    qb = pl.program_id(0)
    q_end = (qb + 1) * tq
    n_kt = (q_end + tk - 1) // tk
    n_full = (qb * tq) // tk
    n_pair = ATTN_HEADS // 2

    for h in range(ATTN_HEADS):
        qs_ref[h * tq:(h + 1) * tq, :] = q_ref[:, h * ATTN_HEAD_DIM:(h + 1) * ATTN_HEAD_DIM]
    for h in range(IDX_HEADS):
        qis_ref[h * tq:(h + 1) * tq, :] = qi_ref[:, h * IDX_HEAD_DIM:(h + 1) * IDX_HEAD_DIM]

    qpos = lax.broadcasted_iota(I32, (1, tq), 1) + qb * tq
    key_lim = (qpos // CHUNK + 1) * CHUNK
    wt = wt_ref[...]

    def score_tile(jt, masked):
        kit = ki_ref[pl.ds(pl.multiple_of(jt * tk, tk), tk), :]
        acc = jnp.zeros((tk, tq), F32)
        for p in range(IDX_HEADS // 2):
            s2 = _dot_nt(kit, qis_ref[2 * p * tq:(2 * p + 2) * tq, :])
            acc = acc + jnp.maximum(s2[:, :tq], 0.0) * wt[2 * p:2 * p + 1, :]
            acc = acc + jnp.maximum(s2[:, tq:], 0.0) * wt[2 * p + 1:2 * p + 2, :]
        bits = pltpu.bitcast(acc, I32)
        key = bits ^ ((bits >> 31) & 0x7FFFFFFF)
        kpos = lax.broadcasted_iota(I32, (tk, tq), 0) + jt * tk
        key = jnp.where(key == 0, -1 - kpos, key)
        if masked:
            key = jnp.where(kpos < key_lim, key, INT_MIN)
        key_ref[jt] = key
        hi_ref[jt] = (key >> 16).astype(I16)
        lo_ref[jt] = (key ^ 0x8000).astype(I16)

    def two_full_tiles(i, c):
        score_tile(2 * i, False)
        score_tile(2 * i + 1, False)
        return c

    def diag_tile(jt, c):
        score_tile(jt, True)
        return c

    lax.fori_loop(0, n_full // 2, two_full_tiles, 0)
    lax.fori_loop(n_full - n_full % 2, n_kt, diag_tile, 0)

    @pl.when(n_kt % 2 == 1)
    def _():
        key_ref[n_kt] = jnp.full((tk, tq), INT_MIN, I32)
        hi_ref[n_kt] = jnp.full((tk, tq), I16_MIN, I16)
        lo_ref[n_kt] = jnp.full((tk, tq), I16_MIN, I16)

    n_kt2 = (n_kt + 1) // 2

    one16 = jnp.ones((tk, tq), I16)
    zero16 = jnp.zeros((tk, tq), I16)

    def fold(cnt, hit):
        hit = hit.reshape(tk // CNT_ROWS, CNT_ROWS, tq)
        for a in range(tk // CNT_ROWS):
            cnt = cnt + hit[a]
        return cnt

    def total(cnt):
        return jnp.sum(cnt.astype(I32), axis=0, keepdims=True)

    def search16(half_ref, need, start=I16_MIN, bits=16):
        def bit_step(i, carry):
            base, got = carry
            cand = base + lax.shift_left(jnp.int32(1), bits - 1 - i)
            cand16 = cand.astype(I16)

            def count_two_tiles(i, cnt):
                cnt = fold(cnt, jnp.where(half_ref[2 * i] >= cand16, one16, zero16))
                return fold(cnt, jnp.where(half_ref[2 * i + 1] >= cand16, one16, zero16))

            tot = total(lax.fori_loop(0, n_kt2, count_two_tiles, jnp.zeros((CNT_ROWS, tq), I16)))
            ok = tot >= need
            return jnp.where(ok, cand, base), jnp.where(ok, tot, got)

        init = (jnp.full((1, tq), start, I32), jnp.full((1, tq), UNKNOWN_COUNT, I32))
        return lax.fori_loop(0, bits, bit_step, init)

    hi_thr, _ = search16(hi_ref, k_sel)
    hi_thr16 = hi_thr.astype(I16)

    def bucket_tile(jt, cnt):
        hi = hi_ref[jt]
        lo_ref[jt] = jnp.where(hi == hi_thr16, lo_ref[jt], jnp.int16(I16_MIN))
        return fold(cnt, jnp.where(hi > hi_thr16, one16, zero16))

    above = total(lax.fori_loop(0, n_kt, bucket_tile, jnp.zeros((CNT_ROWS, tq), I16)))
    lo_thr, reach = search16(lo_ref, k_sel - above)
    thr_raw = hi_thr * 65536 + (lo_thr + 32768)
    thr = jnp.maximum(thr_raw, INT_MIN + 1)

    maybe_tie = ((above + reach) > k_sel) & (thr_raw > INT_MIN)

    @pl.when(jnp.max(jnp.where(maybe_tie, 1, 0)) > 0)
    def _():
        rows = lax.broadcasted_iota(I32, (tk, tq), 0)
        rows16 = rows.astype(I16)
        lo_thr16 = lo_thr.astype(I16)

        def earliness_tile(jt, cnt):
            lo = lo_ref[jt]
            tied = (hi_ref[jt] == hi_thr16) & (lo == lo_thr16)
            hi_ref[jt] = jnp.where(tied, (POS_TOP - jt * tk).astype(I16) - rows16, jnp.int16(I16_MIN))
            return fold(cnt, jnp.where(lo > lo_thr16, one16, zero16))

        beyond = above + total(lax.fori_loop(0, n_kt, earliness_tile, jnp.zeros((CNT_ROWS, tq), I16)))
        keep = jnp.where(maybe_tie, k_sel - beyond, UNKNOWN_COUNT)
        first_kept, _ = search16(hi_ref, keep, start=POS_TOP + 1 - pos_span, bits=pos_span.bit_length() - 1)
        last = POS_TOP - first_kept

        def retire_tile(jt, c):
            key = key_ref[jt]
            drop = (key == thr_raw) & (rows + jt * tk > last) & maybe_tie
            key_ref[jt] = jnp.where(drop, INT_MIN, key)
            return c

        lax.fori_loop(0, n_kt, retire_tile, 0)

    def masked_scores(jt, p, sel2):
        g = p // (n_pair // ATTN_KV_HEADS)
        off = pl.multiple_of(jt * tk, tk)
        kg = k_ref[pl.ds(off, tk), g * ATTN_HEAD_DIM:(g + 1) * ATTN_HEAD_DIM]
        s = _dot_nt(kg, qs_ref[2 * p * tq:(2 * p + 2) * tq, :])
        return s if sel2 is None else jnp.where(sel2, s, NEG_BIG)

    def selection(jt):
        sel = key_ref[jt] >= thr
        return jnp.concatenate([sel, sel], axis=1)

    def values(jt, p):
        g = p // (n_pair // ATTN_KV_HEADS)
        return vt_ref[jt, g * VT_ROWS:(g + 1) * VT_ROWS, :]

    def two_tiles_loop(stage, fold_in, bufs):
        stage(bufs[0], 0)

        def trip(i, c):
            j0 = 2 * i
            stage(bufs[1], j0 + 1)
            fold_in(bufs[0], j0)
            stage(bufs[0], jnp.minimum(j0 + 2, n_kt - 1))
            fold_in(bufs[1], j0 + 1)
            return c

        lax.fori_loop(0, (n_kt + 1) // 2, trip, 0)

    for p in range(n_pair):
        r0 = _col_reduce(masked_scores(0, p, None), jnp.maximum)
        run_ref[p] = r0
        accr_ref[p] = r0
    flag_ref[...] = jnp.full(flag_ref.shape, NEG_BIG, F32)
    acc_ref[...] = jnp.zeros(acc_ref.shape, F32)

    def stage_probs(buf, jt):
        p_ref, r_ref = buf
        sel2 = selection(jt)
        for p in range(n_pair):
            s = masked_scores(jt, p, sel2)
            r = run_ref[p]
            part = None
            for c0 in range(0, tk, 64):
                sc = s[c0:c0 + 64]
                p_ref[p, c0:c0 + 64, :] = jnp.exp2(sc - r).astype(BF16)
                part = sc if part is None else jnp.maximum(part, sc)
            tmax = _col_reduce(part, jnp.maximum)
            r_ref[p] = r
            flag_ref[p] = jnp.maximum(flag_ref[p], tmax - r)
            run_ref[p] = jnp.maximum(r, tmax)

    def fold_probs(buf, jt):
        p_ref, r_ref = buf
        for p in range(n_pair):
            r = r_ref[p]
            acc_ref[p] = acc_ref[p] * jnp.exp2(accr_ref[p] - r) + _dot(values(jt, p), p_ref[p])
            accr_ref[p] = r

    two_tiles_loop(stage_probs, fold_probs, ((p1_ref, r1_ref), (p2_ref, r2_ref)))

    bad = jnp.float32(0.0)
    for p in range(n_pair):
        den = acc_ref[p][ATTN_HEAD_DIM:ATTN_HEAD_DIM + 1]
        ok = (flag_ref[p] < EXP2_HEADROOM) & (den > DEN_MIN) & (den < DEN_MAX)
        bad = jnp.maximum(bad, jnp.max(jnp.where(ok, 0.0, 1.0)))

    @pl.when(bad > 0.0)
    def _():
        m_ref[...] = jnp.full(m_ref.shape, NEG_BIG, F32)
        acc_ref[...] = jnp.zeros(acc_ref.shape, F32)

        def stage_scores(buf, jt):
            s_buf, max_buf = buf
            sel2 = selection(jt)
            for p in range(n_pair):
                s = masked_scores(jt, p, sel2)
                s_buf[p] = s
                max_buf[p] = _col_reduce(s, jnp.maximum)

        def fold_scores(buf, jt):
            s_buf, max_buf = buf
            for p in range(n_pair):
                m_prev = m_ref[p]
                m_cur = jnp.maximum(m_prev, max_buf[p])
                pr = jnp.exp2(s_buf[p] - m_cur).astype(BF16)
                acc_ref[p] = acc_ref[p] * jnp.exp2(m_prev - m_cur) + _dot(values(jt, p), pr)
                m_ref[p] = m_cur

        two_tiles_loop(stage_scores, fold_scores, ((s_ref, smax_ref), (s2_ref, smax2_ref)))

    for p in range(n_pair):
        acc = acc_ref[p]
        o = acc[:ATTN_HEAD_DIM] / acc[ATTN_HEAD_DIM:ATTN_HEAD_DIM + 1]
        for r in range(2):
            h = 2 * p + r
            o_ref[:, h * ATTN_HEAD_DIM:(h + 1) * ATTN_HEAD_DIM] = o[:, r * tq:(r + 1) * tq].T.astype(BF16)


def _dsa(q, qi, wt, k, vt, ki, *, tq=128):
    s = q.shape[0]
    tk = vt.shape[2]
    k_sel = min(TOPK_MAX, s // 4)
    assert s <= POS_TOP + 1 and (s // tk) % 2 == 0 and s % tq == 0
    whole = lambda shape: pl.BlockSpec(shape, lambda i: (0,) * len(shape), pipeline_mode=pl.Buffered(1))
    return pl.pallas_call(
        functools.partial(_dsa_body, tq=tq, tk=tk, k_sel=k_sel, pos_span=1 << (s - 1).bit_length()),
        name="dsa",
        grid=(s // tq,),
        in_specs=[
            pl.BlockSpec((tq, ATTN_WIDTH), lambda i: (i, 0)),
            pl.BlockSpec((tq, 512), lambda i: (i, 0)),
            pl.BlockSpec((IDX_HEADS, tq), lambda i: (0, i)),
            whole((s, KV_WIDTH)),
            whole((s // tk, ATTN_KV_HEADS * VT_ROWS, tk)),
            whole((s, IDX_HEAD_DIM)),
        ],
        out_specs=pl.BlockSpec((tq, ATTN_WIDTH), lambda i: (i, 0)),
        out_shape=jax.ShapeDtypeStruct((s, ATTN_WIDTH), BF16),
        scratch_shapes=[
            pltpu.VMEM((s // tk, tk, tq), I32),
            pltpu.VMEM((s // tk, tk, tq), I16),
            pltpu.VMEM((s // tk, tk, tq), I16),
            pltpu.VMEM((ATTN_HEADS * tq, ATTN_HEAD_DIM), BF16),
            pltpu.VMEM((IDX_HEADS * tq, IDX_HEAD_DIM), BF16),
            pltpu.VMEM((ATTN_HEADS // 2, tk, 2 * tq), BF16),
            pltpu.VMEM((ATTN_HEADS // 2, tk, 2 * tq), BF16),
            pltpu.VMEM((ATTN_HEADS // 2, 1, 2 * tq), F32),
            pltpu.VMEM((ATTN_HEADS // 2, 1, 2 * tq), F32),
            pltpu.VMEM((ATTN_HEADS // 2, 1, 2 * tq), F32),
            pltpu.VMEM((ATTN_HEADS // 2, 1, 2 * tq), F32),
            pltpu.VMEM((ATTN_HEADS // 2, 1, 2 * tq), F32),
            pltpu.VMEM((ATTN_HEADS // 2, tk, 2 * tq), F32),
            pltpu.VMEM((ATTN_HEADS // 2, tk, 2 * tq), F32),
            pltpu.VMEM((ATTN_HEADS // 2, 1, 2 * tq), F32),
            pltpu.VMEM((ATTN_HEADS // 2, 1, 2 * tq), F32),
            pltpu.VMEM((ATTN_HEADS // 2, 1, 2 * tq), F32),
            pltpu.VMEM((ATTN_HEADS // 2, VT_ROWS, 2 * tq), F32),
        ],
        compiler_params=pltpu.CompilerParams(
            dimension_semantics=("parallel",), vmem_limit_bytes=VMEM_LIMIT),
    )(q, qi, wt, k, vt, ki)


def _token_shift(x, prev8, first):
    rolled = pltpu.roll(x, 1, 0)
    prev_row = jnp.where(first, 0.0, prev8[7:8, :])
    row = lax.broadcasted_iota(I32, x.shape, 0)
    return jnp.where(row == 0, jnp.broadcast_to(prev_row, x.shape), rolled)


def _rwkv_prep_body(rr_ref, rk_ref, rv_ref, ms_ref, prr_ref, prk_ref, prv_ref, pms_ref,
                    mur_ref, muk_ref, muv_ref, mum_ref, w0_ref, w2_ref, a0_ref, a2_ref, g2_ref,
                    kk_ref, ka_ref, rkk_ref, bd_ref, tri_ref, blk_ref,
                    mc_ref, nc_ref, yc_ref, y0_ref, bonus_ref, g_ref,
                    at_s, rt_s, bt_s, kt_s, v_s, bdt_s, kdt_s, pct_s):
    first = pl.program_id(0) == 0
    tm = rr_ref.shape[0]

    def lerp(x_ref, p_ref, mu_ref):
        x = x_ref[...]
        return x + (_token_shift(x, p_ref[...], first) - x) * mu_ref[...]

    r = lerp(rr_ref, prr_ref, mur_ref)
    k = lerp(rk_ref, prk_ref, muk_ref)
    v = lerp(rv_ref, prv_ref, muv_ref)
    ms = lerp(ms_ref, pms_ref, mum_ref)
    xw = ms[:, :LANES]
    xag = ms[:, LANES:]

    w = w0_ref[...] + _dot(jnp.tanh(xw).astype(BF16), w2_ref[...])
    z = -w
    softplus = jnp.maximum(z, 0.0) + jnp.log1p(jnp.exp(-jnp.abs(z)))
    lw = -jnp.exp(-softplus - 0.5)
    a_lr = jax.nn.sigmoid(a0_ref[...] + _dot(xag.astype(BF16), a2_ref[...]))
    g_ref[...] = _dot(jax.nn.sigmoid(xag).astype(BF16), g2_ref[...])

    bd = bd_ref[...]
    kk = k * kk_ref[...]
    nrm = jnp.sqrt(_dot_seg(kk * kk, bd))
    kk = kk / jnp.maximum(nrm, 1e-12)
    k = k * (1.0 + (a_lr - 1.0) * ka_ref[...])
    bonus_ref[...] = _dot_seg(r * k * rkk_ref[...], bd) * v

    cum = _dot_exact_lhs(tri_ref[...], lw)
    tot = _dot_exact_lhs(blk_ref[...], lw)
    e_neg = jnp.exp(-cum)
    e_dec = jnp.exp(tot - cum)
    b = kk * a_lr
    gw = RW_GROUP_LANES
    groups = RWKV_WIDTH // gw
    time_major = ((at_s, -kk * jnp.exp(cum - lw)), (rt_s, r * jnp.exp(cum)), (bt_s, b * e_neg),
                  (kt_s, k * e_neg), (v_s, v))
    chan_major = ((bdt_s, (b * e_dec).T), (kdt_s, (k * e_dec).T), (pct_s, jnp.exp(tot).T))
    for g in range(groups):
        for ref, val in time_major:
            ref[g] = val[:, g * gw:(g + 1) * gw]
        for ref, val in chan_major:
            ref[g] = val[g * gw:(g + 1) * gw, :]

    def group_maps(g, c):
        _chunk_maps(at_s.at[g], rt_s.at[g], bt_s.at[g], kt_s.at[g], v_s.at[g],
                    bdt_s.at[g], kdt_s.at[g], pct_s.at[g],
                    mc_ref.at[g], nc_ref.at[g], yc_ref.at[g], y0_ref.at[g],
                    heads=gw // RWKV_HEAD_DIM, chunks=tm // RW_CHUNK)
        return c

    lax.fori_loop(0, groups, group_maps, 0)


def _rwkv_prep(u, p, *, tm=512):
    s = u.shape[0]
    w = RWKV_WIDTH
    t8 = tm // 8
    tile = lambda col: pl.BlockSpec((tm, w), lambda i: (i, col // w))
    prev = lambda col: pl.BlockSpec((8, w), lambda i: (jnp.maximum(i * t8 - 1, 0), col // w))
    vec = lambda n: pl.BlockSpec((1, n), lambda i: (0, 0))
    mat = lambda a, b: pl.BlockSpec((a, b), lambda i: (0, 0))
    tm_spec = pl.BlockSpec((tm, w), lambda i: (i, 0))
    gw = RW_GROUP_LANES
    groups = w // gw
    map_spec = pl.BlockSpec((groups, tm, gw), lambda i: (0, i, 0))
    outs = pl.pallas_call(
        _rwkv_prep_body,
        name="rwkv_prep",
        grid=(s // tm,),
        in_specs=[
            tile(U_RR), tile(U_RK), tile(U_RV),
            pl.BlockSpec((tm, 256), lambda i: (i, U_MISC // 256)),
            prev(U_RR), prev(U_RK), prev(U_RV),
            pl.BlockSpec((8, 256), lambda i: (jnp.maximum(i * t8 - 1, 0), U_MISC // 256)),
            vec(w), vec(w), vec(w), vec(256),
            vec(w), mat(LANES, w), vec(w), mat(LANES, w), mat(LANES, w),
            vec(w), vec(w), vec(w),
            mat(w, w), mat(tm, tm), mat(tm, tm),
        ],
        out_specs=[map_spec] * 4 + [tm_spec] * 2,
        out_shape=[jax.ShapeDtypeStruct((groups, s, gw), F32)] * 4 + [jax.ShapeDtypeStruct((s, w), F32)] * 2,
        scratch_shapes=[pltpu.VMEM((groups, tm, gw), F32)] * 5 + [pltpu.VMEM((groups, gw, tm), F32)] * 3,
        compiler_params=pltpu.CompilerParams(
            dimension_semantics=("parallel",), vmem_limit_bytes=VMEM_LIMIT),
    )(u, u, u, u, u, u, u, u,
      p["mu_r"], p["mu_k"], p["mu_v"], p["mu_m"], p["w0"], p["w2"], p["a0"], p["a2"], p["g2"],
      p["k_k"], p["k_a"], p["r_k"], p["bd"], p["tri"], p["blk"])
    return outs


def _mm1(a, b):
    return _dot(a.astype(BF16), b.astype(BF16))


def _chunk_maps(at_ref, rt_ref, bt_ref, kt_ref, v_ref, bdt_ref, kdt_ref, pct_ref,
                mc_ref, nc_ref, yc_ref, y0_ref, *, heads, chunks):
    n = RWKV_HEAD_DIM
    cl = RW_CHUNK
    ri = lax.broadcasted_iota(I32, (cl, cl), 0)
    ci = lax.broadcasted_iota(I32, (cl, cl), 1)
    strict = ri > ci
    incl = ri >= ci
    same16 = (ri // 16) == (ci // 16)
    eye = (ri == ci).astype(F32)

    pairs = [(slice(cc * cl, (cc + 1) * cl), slice(hh * n, (hh + 1) * n))
             for hh in range(heads) for cc in range(chunks)]
    each = lambda fn, *lists: [fn(*xs) for xs in zip(*lists)]

    at = [at_ref[ts, ls] for ts, ls in pairs]
    rt = [rt_ref[ts, ls] for ts, ls in pairs]
    v = [v_ref[ts, ls] for ts, ls in pairs]
    g = [_mm3_nt(jnp.concatenate([a, r], axis=0), jnp.concatenate([bt_ref[ts, ls], kt_ref[ts, ls]], axis=0))
         for a, r, (ts, ls) in zip(at, rt, pairs)]
    l = [jnp.where(strict, x[:cl, :cl], 0.0) for x in g]
    a_ak = [jnp.where(strict, x[:cl, cl:], 0.0) for x in g]
    a_rb = [jnp.where(incl, x[cl:, :cl], 0.0) for x in g]
    a_rk = [jnp.where(incl, x[cl:, cl:], 0.0) for x in g]

    ld = [jnp.where(same16, x, 0.0) for x in l]
    lo = each(lambda x, y: x - y, l, ld)
    d = [eye + x for x in ld]
    pw = ld
    for _ in range(3):
        pw = each(_mm1, pw, pw)
        d = each(lambda x, y: x + _mm1(x, y), d, pw)
    m = each(_mm1, d, lo)
    m2 = each(_mm1, m, m)
    f = each(lambda x, y: x + _mm1(y, x), d, m2)
    t = each(lambda x, y: x + _mm1(y, x), f, m)

    w1 = each(_mm1, a_ak, v)
    x = each(lambda tt, a, w: _mm1(tt, jnp.concatenate([a, w], axis=1)), t, at, w1)
    big = [_mm1(jnp.concatenate([bdt_ref[ls, ts], arb], axis=0), xx)
           for arb, xx, (ts, ls) in zip(a_rb, x, pairs)]
    kv = [_mm1(jnp.concatenate([kdt_ref[ls, ts], ark], axis=0), vv)
          for ark, vv, (ts, ls) in zip(a_rk, v, pairs)]
    for i, (ts, ls) in enumerate(pairs):
        mc_ref[ts, ls] = eye * pct_ref[ls, ts] + big[i][:n, :n]
        nc_ref[ts, ls] = big[i][:n, n:] + kv[i][:n]
        yc_ref[ts, ls] = rt[i] + big[i][n:, :n]
        y0_ref[ts, ls] = big[i][n:, n:] + kv[i][n:]


def _rwkv_state_body(mc_ref, nc_ref, yc_ref, y0_ref, y_ref, st_ref, *, chunks):
    n = RWKV_HEAD_DIM
    cl = RW_CHUNK
    per_group = RW_GROUP_LANES // n

    @pl.when(pl.program_id(0) == 0)
    def _():
        st_ref[...] = jnp.zeros_like(st_ref)

    heads = [(hh // per_group, slice((hh % per_group) * n, (hh % per_group + 1) * n), slice(hh * n, (hh + 1) * n))
             for hh in range(RWKV_HEADS)]
    st = [st_ref[:, ls] for _, _, ls in heads]
    for cc in range(chunks):
        ts = slice(cc * cl, (cc + 1) * cl)
        r = [_mm3(jnp.concatenate([mc_ref[g, ts, gl], yc_ref[g, ts, gl]], axis=0), s)
             for (g, gl, _), s in zip(heads, st)]
        for (g, gl, ls), x in zip(heads, r):
            y_ref[ts, ls] = x[n:] + y0_ref[g, ts, gl]
        st = [x[:n] + nc_ref[g, ts, gl] for (g, gl, _), x in zip(heads, r)]
    for (_, _, ls), s in zip(heads, st):
        st_ref[:, ls] = s


def _rwkv_state(mc, nc, yc, y0, *, chunks=4):
    groups, s, gw = mc.shape
    w = groups * gw
    rows = chunks * RW_CHUNK
    blk = pl.BlockSpec((groups, rows, gw), lambda c: (0, c, 0))
    return pl.pallas_call(
        functools.partial(_rwkv_state_body, chunks=chunks),
        name="rwkv_state",
        grid=(s // rows,),
        in_specs=[blk] * 4,
        out_specs=pl.BlockSpec((rows, w), lambda c: (c, 0)),
        out_shape=jax.ShapeDtypeStruct((s, w), F32),
        scratch_shapes=[pltpu.VMEM((RWKV_HEAD_DIM, w), F32)],
        compiler_params=pltpu.CompilerParams(
            dimension_semantics=("arbitrary",), vmem_limit_bytes=VMEM_LIMIT),
    )(mc, nc, yc, y0)


def _outproj_body(h_ref, attn_ref, y_ref, bonus_ref, g_ref, gain_ref, bias_ref, bd_ref, wa_ref, wr_ref, o_ref):
    y = y_ref[...]
    bd = bd_ref[...]
    inv_n = 1.0 / RWKV_HEAD_DIM
    mean = _dot_seg(y, bd) * inv_n
    yc = y - mean
    var = _dot_seg(yc * yc, bd) * inv_n
    yn = yc * lax.rsqrt(var + GN_EPS) * gain_ref[...] + bias_ref[...]
    rw = ((yn + bonus_ref[...]) * g_ref[...]).astype(BF16)
    o_ref[...] = h_ref[...] + _dot(attn_ref[...], wa_ref[...]) + _dot(rw, wr_ref[...])


def _outproj(h, attn, y, bonus, g, gain, bias, bd, wa, wr, *, tm=256):
    s, d = h.shape
    w = RWKV_WIDTH
    row = lambda n: pl.BlockSpec((tm, n), lambda i: (i, 0))
    whole = lambda a, b: pl.BlockSpec((a, b), lambda i: (0, 0))
    return pl.pallas_call(
        _outproj_body,
        name="outproj",
        grid=(s // tm,),
        in_specs=[row(d), row(w), row(w), row(w), row(w), whole(1, w), whole(1, w), whole(w, w),
                  whole(ATTN_WIDTH, d), whole(w, d)],
        out_specs=row(d),
        out_shape=jax.ShapeDtypeStruct((s, d), F32),
        compiler_params=pltpu.CompilerParams(
            dimension_semantics=("parallel",), vmem_limit_bytes=VMEM_LIMIT),
    )(h, attn, y, bonus, g, gain, bias, bd, wa, wr)


def _relayout_w_in(w_in):
    d = w_in.shape[0]
    o = np.cumsum([0, 1024, 256, 256, 512, 64, 8, 1024, 1024, 1024, 64, 64, 64])
    seg = lambda i: w_in[:, o[i]:o[i + 1]]
    q, k, v, qi, ki, wi, rr, rk, rv, wd, ad, gd = (seg(i) for i in range(12))
    z = lambda n: jnp.zeros((d, n), w_in.dtype)
    cols = [q, k, v, qi, rr, rk, rv, ki, wd, ad, gd, wi, z(LANES - 8), z(LANES)]
    return jnp.concatenate(cols, axis=1).astype(BF16)


def _pad_rows(w2, lo):
    z = jnp.zeros_like(w2)
    return (jnp.concatenate([w2, z], axis=0) if lo == 0 else jnp.concatenate([z, w2], axis=0)).astype(BF16)


def kernel(x, ffn1_norm, ffn1_w_gate, ffn1_w_up, ffn1_w_down, mix_norm, w_in, w_out, rwkv_mu_r, rwkv_mu_k, rwkv_mu_v, rwkv_mu_w, rwkv_mu_a, rwkv_mu_g, rwkv_w0, rwkv_w2, rwkv_a0, rwkv_a2, rwkv_g2, rwkv_k_k, rwkv_k_a, rwkv_r_k, rwkv_gn_gain, rwkv_gn_bias, ffn2_norm, ffn2_w_gate, ffn2_w_up, ffn2_w_down, final_norm):
    b, s, d = x.shape
    assert b == 1 and d == D_MODEL and ffn1_norm.shape[0] == 1
    h = x[0]
    row = lambda a: a.reshape(1, -1).astype(F32)
    z64 = jnp.zeros((1, LORA), F32)

    tm_prep = 256
    tix = np.arange(tm_prep)
    same_chunk = (tix[:, None] // RW_CHUNK) == (tix[None, :] // RW_CHUNK)
    hix = np.arange(RWKV_WIDTH) // RWKV_HEAD_DIM
    prep = dict(
        mu_r=row(rwkv_mu_r[0]), mu_k=row(rwkv_mu_k[0]), mu_v=row(rwkv_mu_v[0]),
        mu_m=jnp.concatenate([z64, row(rwkv_mu_w[0]), row(rwkv_mu_a[0]), row(rwkv_mu_g[0])], axis=1),
        w0=row(rwkv_w0[0]), w2=_pad_rows(rwkv_w2[0], LORA),
        a0=row(rwkv_a0[0]), a2=_pad_rows(rwkv_a2[0], 0), g2=_pad_rows(rwkv_g2[0], LORA),
        k_k=row(rwkv_k_k[0]), k_a=row(rwkv_k_a[0]), r_k=row(rwkv_r_k[0]),
        bd=jnp.asarray(hix[:, None] == hix[None, :], BF16),
        tri=jnp.asarray(same_chunk & (tix[:, None] >= tix[None, :]), BF16),
        blk=jnp.asarray(same_chunk, BF16),
    )

    h1 = _ffn(h, row(ffn1_norm[0]), ffn1_w_gate[0].astype(BF16), ffn1_w_up[0].astype(BF16),
              ffn1_w_down[0].astype(BF16), row(final_norm), final_norm=False)
    u = _inproj(h1, row(mix_norm[0]), _relayout_w_in(w_in[0]))

    q, k, vt, qi, ki, wt = _attn_prep(u, _rope_tables(s, ATTN_HEAD_DIM), _rope_tables(s, IDX_HEAD_DIM),
                                     tm=DSA_KEY_TILE)
    attn = _dsa(q, qi, wt, k, vt, ki)

    mc, nc, yc, y0, bonus, g = _rwkv_prep(u, prep, tm=tm_prep)
    y = _rwkv_state(mc, nc, yc, y0)

    wo = w_out[0].astype(BF16)
    h2 = _outproj(h1, attn, y, bonus, g, row(rwkv_gn_gain[0]), row(rwkv_gn_bias[0]), prep["bd"],
                  wo[:ATTN_WIDTH], wo[ATTN_WIDTH:])
    out = _ffn(h2, row(ffn2_norm[0]), ffn2_w_gate[0].astype(BF16), ffn2_w_up[0].astype(BF16),
               ffn2_w_down[0].astype(BF16), row(final_norm), final_norm=True)
    return out[None]
```

```python
import functools

import jax
import jax.numpy as jnp
import numpy as np
from jax import lax
from jax.experimental import pallas as pl
from jax.experimental.pallas import tpu as pltpu

F32 = jnp.float32
BF16 = jnp.bfloat16
I32 = jnp.int32

D_MODEL = 2048
CHUNK = 64
ROPE_THETA = 500000.0
ROPE_FRACTION = 4
NORM_EPS = 1e-6
ATTN_HEAD_DIM = 128
ATTN_WIDTH = 1024
ATTN_HEADS = 8
ATTN_KV_HEADS = 2
KV_WIDTH = 256
IDX_HEADS = 8
IDX_HEAD_DIM = 64
TOPK_MAX = 256
RWKV_HEAD_DIM = 64
RWKV_WIDTH = 1024
RWKV_HEADS = 16
LORA = 64
GN_EPS = 64e-5
D_FF = 5632

LANES = 128
VMEM_LIMIT = 56 * 1024 * 1024

U_WIDTH = 5632
U_Q = 0
U_KV = 1024
U_QI = 1536
U_RR = 2048
U_RK = 3072
U_RV = 4096
U_MISC = 5120

RW_CHUNK = 64
RW_GROUP_LANES = 512
DSA_KEY_TILE = 512
VT_ROWS = ATTN_HEAD_DIM + 16
EXP2_HEADROOM = 60.0
DEN_MIN, DEN_MAX = 1e-30, 1e30
CNT_ROWS = 128
I16 = jnp.int16
I16_MIN = -2 ** 15
POS_TOP = 2 ** 15 - 1
UNKNOWN_COUNT = 2 ** 30
LOG2E = 1.4426950408889634
INT_MIN = -2 ** 31
NEG_BIG = -1e30


def _rms(x, g):
    return x * lax.rsqrt(jnp.mean(x * x, axis=-1, keepdims=True) + NORM_EPS) * g


def _dot(a, b):
    return jnp.dot(a, b, preferred_element_type=F32)


def _dot_nt(a, b):
    return lax.dot_general(a, b, (((1,), (1,)), ((), ())), preferred_element_type=F32)


def _split3(x):
    hi = x.astype(BF16)
    r1 = x - hi.astype(F32)
    mid = r1.astype(BF16)
    lo = (r1 - mid.astype(F32)).astype(BF16)
    return hi, mid, lo


def _dot_seg(x, m):
    hi, lo = _split2(x)
    return _dot(hi, m) + _dot(lo, m)


def _dot_exact_lhs(m, x):
    hi, mid, lo = _split3(x)
    return _dot(m, hi) + _dot(m, mid) + _dot(m, lo)


def _split2(x):
    hi = x.astype(BF16)
    lo = (x - hi.astype(F32)).astype(BF16)
    return hi, lo


def _mm3(a, b):
    ah, al = _split2(a)
    bh, bl = _split2(b)
    return _dot(ah, bh) + _dot(ah, bl) + _dot(al, bh)


def _mm3_nt(a, b):
    ah, al = _split2(a)
    bh, bl = _split2(b)
    return _dot_nt(ah, bh) + _dot_nt(ah, bl) + _dot_nt(al, bh)


def _ffn_body(x_ref, g_ref, wg_ref, wu_ref, wd_ref, fg_ref, o_ref, xn_ref, acc_ref, *, final_norm):
    j = pl.program_id(1)

    @pl.when(j == 0)
    def _():
        xn_ref[...] = _rms(x_ref[...], g_ref[...]).astype(BF16)
        acc_ref[...] = jnp.zeros_like(acc_ref)

    xn = xn_ref[...]
    gt = _dot(xn, wg_ref[...])
    ut = _dot(xn, wu_ref[...])
    act = (gt * jax.nn.sigmoid(gt) * ut).astype(BF16)
    acc_ref[...] += _dot(act, wd_ref[...])

    @pl.when(j == pl.num_programs(1) - 1)
    def _():
        h = x_ref[...] + 0.5 * acc_ref[...]
        if final_norm:
            h = _rms(h, fg_ref[...])
        o_ref[...] = h


def _ffn(x, gain, wg, wu, wd, fgain, *, final_norm, tm=512, tf=512):
    s, d = x.shape
    f = wg.shape[1]
    return pl.pallas_call(
        functools.partial(_ffn_body, final_norm=final_norm),
        name="ffn_final" if final_norm else "ffn",
        grid=(s // tm, f // tf),
        in_specs=[
            pl.BlockSpec((tm, d), lambda i, j: (i, 0)),
            pl.BlockSpec((1, d), lambda i, j: (0, 0)),
            pl.BlockSpec((d, tf), lambda i, j: (0, j)),
            pl.BlockSpec((d, tf), lambda i, j: (0, j)),
            pl.BlockSpec((tf, d), lambda i, j: (j, 0)),
            pl.BlockSpec((1, d), lambda i, j: (0, 0)),
        ],
        out_specs=pl.BlockSpec((tm, d), lambda i, j: (i, 0)),
        out_shape=jax.ShapeDtypeStruct((s, d), F32),
        scratch_shapes=[pltpu.VMEM((tm, d), BF16), pltpu.VMEM((tm, d), F32)],
        compiler_params=pltpu.CompilerParams(
            dimension_semantics=("parallel", "arbitrary"), vmem_limit_bytes=VMEM_LIMIT),
    )(x, gain, wg, wu, wd, fgain)


def _inproj_body(x_ref, g_ref, w_ref, o_ref, xn_ref):
    @pl.when(pl.program_id(1) == 0)
    def _():
        xn_ref[...] = _rms(x_ref[...], g_ref[...]).astype(BF16)

    o_ref[...] = _dot(xn_ref[...], w_ref[...])


def _inproj(h, gain, w, *, tm=1024, tn=512):
    s, d = h.shape
    n = w.shape[1]
    return pl.pallas_call(
        _inproj_body,
        name="inproj",
        grid=(s // tm, n // tn),
        in_specs=[
            pl.BlockSpec((tm, d), lambda i, j: (i, 0)),
            pl.BlockSpec((1, d), lambda i, j: (0, 0)),
            pl.BlockSpec((d, tn), lambda i, j: (0, j)),
        ],
        out_specs=pl.BlockSpec((tm, tn), lambda i, j: (i, j)),
        out_shape=jax.ShapeDtypeStruct((s, n), F32),
        scratch_shapes=[pltpu.VMEM((tm, d), BF16)],
        compiler_params=pltpu.CompilerParams(
            dimension_semantics=("parallel", "arbitrary"), vmem_limit_bytes=VMEM_LIMIT),
    )(h, gain, w)


def _rope_tables(s, head_dim):
    rot = head_dim // ROPE_FRACTION
    half = rot // 2
    inv_freq = ROPE_THETA ** (-(jnp.arange(half, dtype=F32) * 2.0 / rot))
    ang = jnp.arange(s, dtype=F32)[:, None] * inv_freq[None, :]
    cos, sin = jnp.cos(ang), jnp.sin(ang)
    ones = jnp.ones((s, head_dim - rot), F32)
    zeros = jnp.zeros((s, head_dim - rot), F32)
    zh = jnp.zeros((s, half), F32)
    c = jnp.concatenate([cos, cos, ones], axis=1)
    sp = jnp.concatenate([zh, sin, zeros], axis=1)
    sm = jnp.concatenate([-sin, zh, zeros], axis=1)
    reps = LANES // head_dim
    return tuple(jnp.tile(t, (1, reps)) for t in (c, sp, sm))


def _rope128(x, c, sp, sm, half):
    return x * c + pltpu.roll(x, half, 1) * sp + pltpu.roll(x, LANES - half, 1) * sm


def _attn_prep_body(q_ref, kv_ref, qi_ref, misc_ref, c1_ref, sp1_ref, sm1_ref, c2_ref, sp2_ref, sm2_ref,
                    qo_ref, ko_ref, vo_ref, qio_ref, kio_ref, wio_ref):
    c1, sp1, sm1 = c1_ref[...], sp1_ref[...], sm1_ref[...]
    c2, sp2, sm2 = c2_ref[...], sp2_ref[...], sm2_ref[...]
    att_scale = ATTN_HEAD_DIM ** -0.5 * LOG2E
    for b in range(ATTN_WIDTH // LANES):
        sl = slice(b * LANES, (b + 1) * LANES)
        qo_ref[:, sl] = (_rope128(q_ref[:, sl], c1, sp1, sm1, 16) * att_scale).astype(BF16)
    for b in range(KV_WIDTH // LANES):
        sl = slice(b * LANES, (b + 1) * LANES)
        ko_ref[:, sl] = _rope128(kv_ref[:, sl], c1, sp1, sm1, 16).astype(BF16)
    vt = kv_ref[:, KV_WIDTH:].T.astype(BF16)
    ones = jnp.ones((VT_ROWS - ATTN_HEAD_DIM, vt.shape[1]), BF16)
    for g in range(ATTN_KV_HEADS):
        vo_ref[0, g * VT_ROWS:g * VT_ROWS + ATTN_HEAD_DIM, :] = vt[g * ATTN_HEAD_DIM:(g + 1) * ATTN_HEAD_DIM]
        vo_ref[0, g * VT_ROWS + ATTN_HEAD_DIM:(g + 1) * VT_ROWS, :] = ones
    for b in range(IDX_HEADS * IDX_HEAD_DIM // LANES):
        sl = slice(b * LANES, (b + 1) * LANES)
        qio_ref[:, sl] = _rope128(qi_ref[:, sl], c2, sp2, sm2, 8).astype(BF16)
    ki = _rope128(misc_ref[:, :LANES], c2, sp2, sm2, 8)
    kio_ref[...] = ki[:, :IDX_HEAD_DIM].astype(BF16)
    idx_scale = (IDX_HEADS ** -0.5) * (IDX_HEAD_DIM ** -0.5)
    wio_ref[...] = (misc_ref[:, 2 * LANES:3 * LANES] * idx_scale).T[:IDX_HEADS]


def _attn_prep(u, tabs128, tabs64, *, tm):
    s = u.shape[0]
    tab_spec = pl.BlockSpec((tm, LANES), lambda i: (i, 0))
    return pl.pallas_call(
        _attn_prep_body,
        name="attn_prep",
        grid=(s // tm,),
        in_specs=[
            pl.BlockSpec((tm, ATTN_WIDTH), lambda i: (i, U_Q // ATTN_WIDTH)),
            pl.BlockSpec((tm, 512), lambda i: (i, U_KV // 512)),
            pl.BlockSpec((tm, 512), lambda i: (i, U_QI // 512)),
            pl.BlockSpec((tm, 512), lambda i: (i, U_MISC // 512)),
        ] + [tab_spec] * 6,
        out_specs=[
            pl.BlockSpec((tm, ATTN_WIDTH), lambda i: (i, 0)),
            pl.BlockSpec((tm, KV_WIDTH), lambda i: (i, 0)),
            pl.BlockSpec((1, ATTN_KV_HEADS * VT_ROWS, tm), lambda i: (i, 0, 0)),
            pl.BlockSpec((tm, 512), lambda i: (i, 0)),
            pl.BlockSpec((tm, IDX_HEAD_DIM), lambda i: (i, 0)),
            pl.BlockSpec((IDX_HEADS, tm), lambda i: (0, i)),
        ],
        out_shape=[
            jax.ShapeDtypeStruct((s, ATTN_WIDTH), BF16),
            jax.ShapeDtypeStruct((s, KV_WIDTH), BF16),
            jax.ShapeDtypeStruct((s // tm, ATTN_KV_HEADS * VT_ROWS, tm), BF16),
            jax.ShapeDtypeStruct((s, 512), BF16),
            jax.ShapeDtypeStruct((s, IDX_HEAD_DIM), BF16),
            jax.ShapeDtypeStruct((IDX_HEADS, s), F32),
        ],
        compiler_params=pltpu.CompilerParams(
            dimension_semantics=("parallel",), vmem_limit_bytes=VMEM_LIMIT),
    )(u, u, u, u, *tabs128, *tabs64)


def _col_reduce(x, op):
    n, w = x.shape
    y = x.reshape(n // 64, 64, w)
    acc = y[0]
    for a in range(1, n // 64):
        acc = op(acc, y[a])
    z = acc.reshape(8, 8, w)
    t = [op(z[2 * i], z[2 * i + 1]) for i in range(4)]
    r = op(op(t[0], t[1]), op(t[2], t[3]))
    red = jnp.max if op is jnp.maximum else jnp.sum
    return red(r, axis=0, keepdims=True)


def _dsa_body(q_ref, qi_ref, wt_ref, k_ref, vt_ref, ki_ref, o_ref,
              key_ref, hi_ref, lo_ref, qs_ref, qis_ref, p1_ref, p2_ref, r1_ref, r2_ref, run_ref, accr_ref,
              flag_ref, s_ref, s2_ref, smax_ref, smax2_ref, m_ref, acc_ref, *, tq, tk, k_sel, pos_span):
    qb = pl.program_id(0)
    q_end = (qb + 1) * tq
    n_kt = (q_end + tk - 1) // tk
    n_full = (qb * tq) // tk
    n_pair = ATTN_HEADS // 2

    for h in range(ATTN_HEADS):
        qs_ref[h * tq:(h + 1) * tq, :] = q_ref[:, h * ATTN_HEAD_DIM:(h + 1) * ATTN_HEAD_DIM]
    for h in range(IDX_HEADS):
        qis_ref[h * tq:(h + 1) * tq, :] = qi_ref[:, h * IDX_HEAD_DIM:(h + 1) * IDX_HEAD_DIM]

    qpos = lax.broadcasted_iota(I32, (1, tq), 1) + qb * tq
    key_lim = (qpos // CHUNK + 1) * CHUNK
    wt = wt_ref[...]

    def score_tile(jt, masked):
        kit = ki_ref[pl.ds(pl.multiple_of(jt * tk, tk), tk), :]
        acc = jnp.zeros((tk, tq), F32)
        for p in range(IDX_HEADS // 2):
            s2 = _dot_nt(kit, qis_ref[2 * p * tq:(2 * p + 2) * tq, :])
            acc = acc + jnp.maximum(s2[:, :tq], 0.0) * wt[2 * p:2 * p + 1, :]
            acc = acc + jnp.maximum(s2[:, tq:], 0.0) * wt[2 * p + 1:2 * p + 2, :]
        bits = pltpu.bitcast(acc, I32)
        key = bits ^ ((bits >> 31) & 0x7FFFFFFF)
        kpos = lax.broadcasted_iota(I32, (tk, tq), 0) + jt * tk
        key = jnp.where(key == 0, -1 - kpos, key)
        if masked:
            key = jnp.where(kpos < key_lim, key, INT_MIN)
        key_ref[jt] = key
        hi_ref[jt] = (key >> 16).astype(I16)
        lo_ref[jt] = (key ^ 0x8000).astype(I16)

    def two_full_tiles(i, c):
        score_tile(2 * i, False)
        score_tile(2 * i + 1, False)
        return c

    def diag_tile(jt, c):
        score_tile(jt, True)
        return c

    lax.fori_loop(0, n_full // 2, two_full_tiles, 0)
    lax.fori_loop(n_full - n_full % 2, n_kt, diag_tile, 0)

    @pl.when(n_kt % 2 == 1)
    def _():
        key_ref[n_kt] = jnp.full((tk, tq), INT_MIN, I32)
        hi_ref[n_kt] = jnp.full((tk, tq), I16_MIN, I16)
        lo_ref[n_kt] = jnp.full((tk, tq), I16_MIN, I16)

    n_kt2 = (n_kt + 1) // 2

    one16 = jnp.ones((tk, tq), I16)
    zero16 = jnp.zeros((tk, tq), I16)

    def fold(cnt, hit):
        hit = hit.reshape(tk // CNT_ROWS, CNT_ROWS, tq)
        for a in range(tk // CNT_ROWS):
            cnt = cnt + hit[a]
        return cnt

    def total(cnt):
        return jnp.sum(cnt.astype(I32), axis=0, keepdims=True)

    def search16(half_ref, need, start=I16_MIN, bits=16):
        def bit_step(i, carry):
            base, got = carry
            cand = base + lax.shift_left(jnp.int32(1), bits - 1 - i)
            cand16 = cand.astype(I16)

            def count_two_tiles(i, cnt):
                cnt = fold(cnt, jnp.where(half_ref[2 * i] >= cand16, one16, zero16))
                return fold(cnt, jnp.where(half_ref[2 * i + 1] >= cand16, one16, zero16))

            tot = total(lax.fori_loop(0, n_kt2, count_two_tiles, jnp.zeros((CNT_ROWS, tq), I16)))
            ok = tot >= need
            return jnp.where(ok, cand, base), jnp.where(ok, tot, got)

        init = (jnp.full((1, tq), start, I32), jnp.full((1, tq), UNKNOWN_COUNT, I32))
        return lax.fori_loop(0, bits, bit_step, init)

    hi_thr, _ = search16(hi_ref, k_sel)
    hi_thr16 = hi_thr.astype(I16)

    def bucket_tile(jt, cnt):
        hi = hi_ref[jt]
        lo_ref[jt] = jnp.where(hi == hi_thr16, lo_ref[jt], jnp.int16(I16_MIN))
        return fold(cnt, jnp.where(hi > hi_thr16, one16, zero16))

    above = total(lax.fori_loop(0, n_kt, bucket_tile, jnp.zeros((CNT_ROWS, tq), I16)))
    lo_thr, reach = search16(lo_ref, k_sel - above)
    thr_raw = hi_thr * 65536 + (lo_thr + 32768)
    thr = jnp.maximum(thr_raw, INT_MIN + 1)

    maybe_tie = ((above + reach) > k_sel) & (thr_raw > INT_MIN)

    @pl.when(jnp.max(jnp.where(maybe_tie, 1, 0)) > 0)
    def _():
        rows = lax.broadcasted_iota(I32, (tk, tq), 0)
        rows16 = rows.astype(I16)
        lo_thr16 = lo_thr.astype(I16)

        def earliness_tile(jt, cnt):
            lo = lo_ref[jt]
            tied = (hi_ref[jt] == hi_thr16) & (lo == lo_thr16)
            hi_ref[jt] = jnp.where(tied, (POS_TOP - jt * tk).astype(I16) - rows16, jnp.int16(I16_MIN))
            return fold(cnt, jnp.where(lo > lo_thr16, one16, zero16))

        beyond = above + total(lax.fori_loop(0, n_kt, earliness_tile, jnp.zeros((CNT_ROWS, tq), I16)))
        keep = jnp.where(maybe_tie, k_sel - beyond, UNKNOWN_COUNT)
        first_kept, _ = search16(hi_ref, keep, start=POS_TOP + 1 - pos_span, bits=pos_span.bit_length() - 1)
        last = POS_TOP - first_kept

        def retire_tile(jt, c):
            key = key_ref[jt]
            drop = (key == thr_raw) & (rows + jt * tk > last) & maybe_tie
            key_ref[jt] = jnp.where(drop, INT_MIN, key)
            return c

        lax.fori_loop(0, n_kt, retire_tile, 0)

    def masked_scores(jt, p, sel2):
        g = p // (n_pair // ATTN_KV_HEADS)
        off = pl.multiple_of(jt * tk, tk)
        kg = k_ref[pl.ds(off, tk), g * ATTN_HEAD_DIM:(g + 1) * ATTN_HEAD_DIM]
        s = _dot_nt(kg, qs_ref[2 * p * tq:(2 * p + 2) * tq, :])
        return s if sel2 is None else jnp.where(sel2, s, NEG_BIG)

    def selection(jt):
        sel = key_ref[jt] >= thr
        return jnp.concatenate([sel, sel], axis=1)

    def values(jt, p):
        g = p // (n_pair // ATTN_KV_HEADS)
        return vt_ref[jt, g * VT_ROWS:(g + 1) * VT_ROWS, :]

    def two_tiles_loop(stage, fold_in, bufs):
        stage(bufs[0], 0)

        def trip(i, c):
            j0 = 2 * i
            stage(bufs[1], j0 + 1)
            fold_in(bufs[0], j0)
            stage(bufs[0], jnp.minimum(j0 + 2, n_kt - 1))
            fold_in(bufs[1], j0 + 1)
            return c

        lax.fori_loop(0, (n_kt + 1) // 2, trip, 0)

    run_ref[...] = jnp.zeros(run_ref.shape, F32)
    accr_ref[...] = jnp.zeros(accr_ref.shape, F32)
    flag_ref[...] = jnp.full(flag_ref.shape, NEG_BIG, F32)
    acc_ref[...] = jnp.zeros(acc_ref.shape, F32)

    def stage_probs(buf, jt):
        p_ref, r_ref = buf
        sel2 = selection(jt)
        for p in range(n_pair):
            s = masked_scores(jt, p, sel2)
            r = run_ref[p]
            part = None
            for c0 in range(0, tk, 64):
                sc = s[c0:c0 + 64]
                p_ref[p, c0:c0 + 64, :] = jnp.exp2(sc - r).astype(BF16)
                part = sc if part is None else jnp.maximum(part, sc)
            tmax = _col_reduce(part, jnp.maximum)
            r_ref[p] = r
            flag_ref[p] = jnp.maximum(flag_ref[p], tmax - r)
            run_ref[p] = jnp.maximum(r, tmax)

    def fold_probs(buf, jt):
        p_ref, r_ref = buf
        for p in range(n_pair):
            r = r_ref[p]
            acc_ref[p] = acc_ref[p] * jnp.exp2(accr_ref[p] - r) + _dot(values(jt, p), p_ref[p])
            accr_ref[p] = r

    two_tiles_loop(stage_probs, fold_probs, ((p1_ref, r1_ref), (p2_ref, r2_ref)))

    bad = jnp.float32(0.0)
    for p in range(n_pair):
        den = acc_ref[p][ATTN_HEAD_DIM:ATTN_HEAD_DIM + 1]
        ok = (flag_ref[p] < EXP2_HEADROOM) & (den > DEN_MIN) & (den < DEN_MAX)
        bad = jnp.maximum(bad, jnp.max(jnp.where(ok, 0.0, 1.0)))

    @pl.when(bad > 0.0)
    def _():
        m_ref[...] = jnp.full(m_ref.shape, NEG_BIG, F32)
        acc_ref[...] = jnp.zeros(acc_ref.shape, F32)

        def stage_scores(buf, jt):
            s_buf, max_buf = buf
            sel2 = selection(jt)
            for p in range(n_pair):
                s = masked_scores(jt, p, sel2)
                s_buf[p] = s
                max_buf[p] = _col_reduce(s, jnp.maximum)

        def fold_scores(buf, jt):
            s_buf, max_buf = buf
            for p in range(n_pair):
                m_prev = m_ref[p]
                m_cur = jnp.maximum(m_prev, max_buf[p])
                pr = jnp.exp2(s_buf[p] - m_cur).astype(BF16)
                acc_ref[p] = acc_ref[p] * jnp.exp2(m_prev - m_cur) + _dot(values(jt, p), pr)
                m_ref[p] = m_cur

        two_tiles_loop(stage_scores, fold_scores, ((s_ref, smax_ref), (s2_ref, smax2_ref)))

    for p in range(n_pair):
        acc = acc_ref[p]
        o = acc[:ATTN_HEAD_DIM] / acc[ATTN_HEAD_DIM:ATTN_HEAD_DIM + 1]
        for r in range(2):
            h = 2 * p + r
            o_ref[:, h * ATTN_HEAD_DIM:(h + 1) * ATTN_HEAD_DIM] = o[:, r * tq:(r + 1) * tq].T.astype(BF16)


def _dsa(q, qi, wt, k, vt, ki, *, tq=128):
    s = q.shape[0]
    tk = vt.shape[2]
    k_sel = min(TOPK_MAX, s // 4)
    assert s <= POS_TOP + 1 and (s // tk) % 2 == 0 and s % tq == 0
    whole = lambda shape: pl.BlockSpec(shape, lambda i: (0,) * len(shape), pipeline_mode=pl.Buffered(1))
    return pl.pallas_call(
        functools.partial(_dsa_body, tq=tq, tk=tk, k_sel=k_sel, pos_span=1 << (s - 1).bit_length()),
        name="dsa",
        grid=(s // tq,),
        in_specs=[
            pl.BlockSpec((tq, ATTN_WIDTH), lambda i: (i, 0)),
            pl.BlockSpec((tq, 512), lambda i: (i, 0)),
            pl.BlockSpec((IDX_HEADS, tq), lambda i: (0, i)),
            whole((s, KV_WIDTH)),
            whole((s // tk, ATTN_KV_HEADS * VT_ROWS, tk)),
            whole((s, IDX_HEAD_DIM)),
        ],
        out_specs=pl.BlockSpec((tq, ATTN_WIDTH), lambda i: (i, 0)),
        out_shape=jax.ShapeDtypeStruct((s, ATTN_WIDTH), BF16),
        scratch_shapes=[
            pltpu.VMEM((s // tk, tk, tq), I32),
            pltpu.VMEM((s // tk, tk, tq), I16),
            pltpu.VMEM((s // tk, tk, tq), I16),
            pltpu.VMEM((ATTN_HEADS * tq, ATTN_HEAD_DIM), BF16),
            pltpu.VMEM((IDX_HEADS * tq, IDX_HEAD_DIM), BF16),
            pltpu.VMEM((ATTN_HEADS // 2, tk, 2 * tq), BF16),
            pltpu.VMEM((ATTN_HEADS // 2, tk, 2 * tq), BF16),
            pltpu.VMEM((ATTN_HEADS // 2, 1, 2 * tq), F32),
            pltpu.VMEM((ATTN_HEADS // 2, 1, 2 * tq), F32),
            pltpu.VMEM((ATTN_HEADS // 2, 1, 2 * tq), F32),
            pltpu.VMEM((ATTN_HEADS // 2, 1, 2 * tq), F32),
            pltpu.VMEM((ATTN_HEADS // 2, 1, 2 * tq), F32),
            pltpu.VMEM((ATTN_HEADS // 2, tk, 2 * tq), F32),
            pltpu.VMEM((ATTN_HEADS // 2, tk, 2 * tq), F32),
            pltpu.VMEM((ATTN_HEADS // 2, 1, 2 * tq), F32),
            pltpu.VMEM((ATTN_HEADS // 2, 1, 2 * tq), F32),
            pltpu.VMEM((ATTN_HEADS // 2, 1, 2 * tq), F32),
            pltpu.VMEM((ATTN_HEADS // 2, VT_ROWS, 2 * tq), F32),
        ],
        compiler_params=pltpu.CompilerParams(
            dimension_semantics=("parallel",), vmem_limit_bytes=VMEM_LIMIT),
    )(q, qi, wt, k, vt, ki)


def _token_shift(x, prev8, first):
    rolled = pltpu.roll(x, 1, 0)
    prev_row = jnp.where(first, 0.0, prev8[7:8, :])
    row = lax.broadcasted_iota(I32, x.shape, 0)
    return jnp.where(row == 0, jnp.broadcast_to(prev_row, x.shape), rolled)


def _rwkv_prep_body(rr_ref, rk_ref, rv_ref, ms_ref, prr_ref, prk_ref, prv_ref, pms_ref,
                    mur_ref, muk_ref, muv_ref, mum_ref, w0_ref, w2_ref, a0_ref, a2_ref, g2_ref,
                    kk_ref, ka_ref, rkk_ref, bd_ref, tri_ref, blk_ref,
                    mc_ref, nc_ref, yc_ref, y0_ref, bonus_ref, g_ref,
                    at_s, rt_s, bt_s, kt_s, v_s, bdt_s, kdt_s, pct_s):
    first = pl.program_id(0) == 0
    tm = rr_ref.shape[0]

    def lerp(x_ref, p_ref, mu_ref):
        x = x_ref[...]
        return x + (_token_shift(x, p_ref[...], first) - x) * mu_ref[...]

    r = lerp(rr_ref, prr_ref, mur_ref)
    k = lerp(rk_ref, prk_ref, muk_ref)
    v = lerp(rv_ref, prv_ref, muv_ref)
    ms = lerp(ms_ref, pms_ref, mum_ref)
    xw = ms[:, :LANES]
    xag = ms[:, LANES:]

    w = w0_ref[...] + _dot(jnp.tanh(xw).astype(BF16), w2_ref[...])
    z = -w
    softplus = jnp.maximum(z, 0.0) + jnp.log1p(jnp.exp(-jnp.abs(z)))
    lw = -jnp.exp(-softplus - 0.5)
    a_lr = jax.nn.sigmoid(a0_ref[...] + _dot(xag.astype(BF16), a2_ref[...]))
    g_ref[...] = _dot(jax.nn.sigmoid(xag).astype(BF16), g2_ref[...])

    bd = bd_ref[...]
    kk = k * kk_ref[...]
    nrm = jnp.sqrt(_dot_seg(kk * kk, bd))
    kk = kk / jnp.maximum(nrm, 1e-12)
    k = k * (1.0 + (a_lr - 1.0) * ka_ref[...])
    bonus_ref[...] = _dot_seg(r * k * rkk_ref[...], bd) * v

    cum = _dot_exact_lhs(tri_ref[...], lw)
    tot = _dot_exact_lhs(blk_ref[...], lw)
    e_neg = jnp.exp(-cum)
    e_dec = jnp.exp(tot - cum)
    b = kk * a_lr
    gw = RW_GROUP_LANES
    groups = RWKV_WIDTH // gw
    time_major = ((at_s, -kk * jnp.exp(cum - lw)), (rt_s, r * jnp.exp(cum)), (bt_s, b * e_neg),
                  (kt_s, k * e_neg), (v_s, v))
    chan_major = ((bdt_s, (b * e_dec).T), (kdt_s, (k * e_dec).T), (pct_s, jnp.exp(tot).T))
    for g in range(groups):
        for ref, val in time_major:
            ref[g] = val[:, g * gw:(g + 1) * gw]
        for ref, val in chan_major:
            ref[g] = val[g * gw:(g + 1) * gw, :]

    def group_maps(g, c):
        _chunk_maps(at_s.at[g], rt_s.at[g], bt_s.at[g], kt_s.at[g], v_s.at[g],
                    bdt_s.at[g], kdt_s.at[g], pct_s.at[g],
                    mc_ref.at[g], nc_ref.at[g], yc_ref.at[g], y0_ref.at[g],
                    heads=gw // RWKV_HEAD_DIM, chunks=tm // RW_CHUNK)
        return c

    lax.fori_loop(0, groups, group_maps, 0)


def _rwkv_prep(u, p, *, tm=512):
    s = u.shape[0]
    w = RWKV_WIDTH
    t8 = tm // 8
    tile = lambda col: pl.BlockSpec((tm, w), lambda i: (i, col // w))
    prev = lambda col: pl.BlockSpec((8, w), lambda i: (jnp.maximum(i * t8 - 1, 0), col // w))
    vec = lambda n: pl.BlockSpec((1, n), lambda i: (0, 0))
    mat = lambda a, b: pl.BlockSpec((a, b), lambda i: (0, 0))
    tm_spec = pl.BlockSpec((tm, w), lambda i: (i, 0))
    gw = RW_GROUP_LANES
    groups = w // gw
    map_spec = pl.BlockSpec((groups, tm, gw), lambda i: (0, i, 0))
    outs = pl.pallas_call(
        _rwkv_prep_body,
        name="rwkv_prep",
        grid=(s // tm,),
        in_specs=[
            tile(U_RR), tile(U_RK), tile(U_RV),
            pl.BlockSpec((tm, 256), lambda i: (i, U_MISC // 256)),
            prev(U_RR), prev(U_RK), prev(U_RV),
            pl.BlockSpec((8, 256), lambda i: (jnp.maximum(i * t8 - 1, 0), U_MISC // 256)),
            vec(w), vec(w), vec(w), vec(256),
            vec(w), mat(LANES, w), vec(w), mat(LANES, w), mat(LANES, w),
            vec(w), vec(w), vec(w),
            mat(w, w), mat(tm, tm), mat(tm, tm),
        ],
        out_specs=[map_spec] * 4 + [tm_spec] * 2,
        out_shape=[jax.ShapeDtypeStruct((groups, s, gw), F32)] * 4 + [jax.ShapeDtypeStruct((s, w), F32)] * 2,
        scratch_shapes=[pltpu.VMEM((groups, tm, gw), F32)] * 5 + [pltpu.VMEM((groups, gw, tm), F32)] * 3,
        compiler_params=pltpu.CompilerParams(
            dimension_semantics=("parallel",), vmem_limit_bytes=VMEM_LIMIT),
    )(u, u, u, u, u, u, u, u,
      p["mu_r"], p["mu_k"], p["mu_v"], p["mu_m"], p["w0"], p["w2"], p["a0"], p["a2"], p["g2"],
      p["k_k"], p["k_a"], p["r_k"], p["bd"], p["tri"], p["blk"])
    return outs


def _mm1(a, b):
    return _dot(a.astype(BF16), b.astype(BF16))


def _chunk_maps(at_ref, rt_ref, bt_ref, kt_ref, v_ref, bdt_ref, kdt_ref, pct_ref,
                mc_ref, nc_ref, yc_ref, y0_ref, *, heads, chunks):
    n = RWKV_HEAD_DIM
    cl = RW_CHUNK
    ri = lax.broadcasted_iota(I32, (cl, cl), 0)
    ci = lax.broadcasted_iota(I32, (cl, cl), 1)
    strict = ri > ci
    incl = ri >= ci
    same16 = (ri // 16) == (ci // 16)
    eye = (ri == ci).astype(F32)

    pairs = [(slice(cc * cl, (cc + 1) * cl), slice(hh * n, (hh + 1) * n))
             for hh in range(heads) for cc in range(chunks)]
    each = lambda fn, *lists: [fn(*xs) for xs in zip(*lists)]

    at = [at_ref[ts, ls] for ts, ls in pairs]
    rt = [rt_ref[ts, ls] for ts, ls in pairs]
    v = [v_ref[ts, ls] for ts, ls in pairs]
    g = [_mm3_nt(jnp.concatenate([a, r], axis=0), jnp.concatenate([bt_ref[ts, ls], kt_ref[ts, ls]], axis=0))
         for a, r, (ts, ls) in zip(at, rt, pairs)]
    l = [jnp.where(strict, x[:cl, :cl], 0.0) for x in g]
    a_ak = [jnp.where(strict, x[:cl, cl:], 0.0) for x in g]
    a_rb = [jnp.where(incl, x[cl:, :cl], 0.0) for x in g]
    a_rk = [jnp.where(incl, x[cl:, cl:], 0.0) for x in g]

    ld = [jnp.where(same16, x, 0.0) for x in l]
    lo = each(lambda x, y: x - y, l, ld)
    d = [eye + x for x in ld]
    pw = ld
    for _ in range(3):
        pw = each(_mm1, pw, pw)
        d = each(lambda x, y: x + _mm1(x, y), d, pw)
    m = each(_mm1, d, lo)
    m2 = each(_mm1, m, m)
    f = each(lambda x, y: x + _mm1(y, x), d, m2)
    t = each(lambda x, y: x + _mm1(y, x), f, m)

    w1 = each(_mm1, a_ak, v)
    x = each(lambda tt, a, w: _mm1(tt, jnp.concatenate([a, w], axis=1)), t, at, w1)
    big = [_mm1(jnp.concatenate([bdt_ref[ls, ts], arb], axis=0), xx)
           for arb, xx, (ts, ls) in zip(a_rb, x, pairs)]
    kv = [_mm1(jnp.concatenate([kdt_ref[ls, ts], ark], axis=0), vv)
          for ark, vv, (ts, ls) in zip(a_rk, v, pairs)]
    for i, (ts, ls) in enumerate(pairs):
        mc_ref[ts, ls] = eye * pct_ref[ls, ts] + big[i][:n, :n]
        nc_ref[ts, ls] = big[i][:n, n:] + kv[i][:n]
        yc_ref[ts, ls] = rt[i] + big[i][n:, :n]
        y0_ref[ts, ls] = big[i][n:, n:] + kv[i][n:]


def _rwkv_state_body(mc_ref, nc_ref, yc_ref, y0_ref, y_ref, st_ref, *, chunks):
    n = RWKV_HEAD_DIM
    cl = RW_CHUNK
    per_group = RW_GROUP_LANES // n

    @pl.when(pl.program_id(0) == 0)
    def _():
        st_ref[...] = jnp.zeros_like(st_ref)

    heads = [(hh // per_group, slice((hh % per_group) * n, (hh % per_group + 1) * n), slice(hh * n, (hh + 1) * n))
             for hh in range(RWKV_HEADS)]
    st = [st_ref[:, ls] for _, _, ls in heads]
    for cc in range(chunks):
        ts = slice(cc * cl, (cc + 1) * cl)
        r = [_mm3(jnp.concatenate([mc_ref[g, ts, gl], yc_ref[g, ts, gl]], axis=0), s)
             for (g, gl, _), s in zip(heads, st)]
        for (g, gl, ls), x in zip(heads, r):
            y_ref[ts, ls] = x[n:] + y0_ref[g, ts, gl]
        st = [x[:n] + nc_ref[g, ts, gl] for (g, gl, _), x in zip(heads, r)]
    for (_, _, ls), s in zip(heads, st):
        st_ref[:, ls] = s


def _rwkv_state(mc, nc, yc, y0, *, chunks=4):
    groups, s, gw = mc.shape
    w = groups * gw
    rows = chunks * RW_CHUNK
    blk = pl.BlockSpec((groups, rows, gw), lambda c: (0, c, 0))
    return pl.pallas_call(
        functools.partial(_rwkv_state_body, chunks=chunks),
        name="rwkv_state",
        grid=(s // rows,),
        in_specs=[blk] * 4,
        out_specs=pl.BlockSpec((rows, w), lambda c: (c, 0)),
        out_shape=jax.ShapeDtypeStruct((s, w), F32),
        scratch_shapes=[pltpu.VMEM((RWKV_HEAD_DIM, w), F32)],
        compiler_params=pltpu.CompilerParams(
            dimension_semantics=("arbitrary",), vmem_limit_bytes=VMEM_LIMIT),
    )(mc, nc, yc, y0)


def _outproj_body(h_ref, attn_ref, y_ref, bonus_ref, g_ref, gain_ref, bias_ref, bd_ref, wa_ref, wr_ref, o_ref):
    y = y_ref[...]
    bd = bd_ref[...]
    inv_n = 1.0 / RWKV_HEAD_DIM
    mean = _dot_seg(y, bd) * inv_n
    yc = y - mean
    var = _dot_seg(yc * yc, bd) * inv_n
    yn = yc * lax.rsqrt(var + GN_EPS) * gain_ref[...] + bias_ref[...]
    rw = ((yn + bonus_ref[...]) * g_ref[...]).astype(BF16)
    o_ref[...] = h_ref[...] + _dot(attn_ref[...], wa_ref[...]) + _dot(rw, wr_ref[...])


def _outproj(h, attn, y, bonus, g, gain, bias, bd, wa, wr, *, tm=256):
    s, d = h.shape
    w = RWKV_WIDTH
    row = lambda n: pl.BlockSpec((tm, n), lambda i: (i, 0))
    whole = lambda a, b: pl.BlockSpec((a, b), lambda i: (0, 0))
    return pl.pallas_call(
        _outproj_body,
        name="outproj",
        grid=(s // tm,),
        in_specs=[row(d), row(w), row(w), row(w), row(w), whole(1, w), whole(1, w), whole(w, w),
                  whole(ATTN_WIDTH, d), whole(w, d)],
        out_specs=row(d),
        out_shape=jax.ShapeDtypeStruct((s, d), F32),
        compiler_params=pltpu.CompilerParams(
            dimension_semantics=("parallel",), vmem_limit_bytes=VMEM_LIMIT),
    )(h, attn, y, bonus, g, gain, bias, bd, wa, wr)


def _relayout_w_in(w_in):
    d = w_in.shape[0]
    o = np.cumsum([0, 1024, 256, 256, 512, 64, 8, 1024, 1024, 1024, 64, 64, 64])
    seg = lambda i: w_in[:, o[i]:o[i + 1]]
    q, k, v, qi, ki, wi, rr, rk, rv, wd, ad, gd = (seg(i) for i in range(12))
    z = lambda n: jnp.zeros((d, n), w_in.dtype)
    cols = [q, k, v, qi, rr, rk, rv, ki, wd, ad, gd, wi, z(LANES - 8), z(LANES)]
    return jnp.concatenate(cols, axis=1).astype(BF16)


def _pad_rows(w2, lo):
    z = jnp.zeros_like(w2)
    return (jnp.concatenate([w2, z], axis=0) if lo == 0 else jnp.concatenate([z, w2], axis=0)).astype(BF16)


def kernel(x, ffn1_norm, ffn1_w_gate, ffn1_w_up, ffn1_w_down, mix_norm, w_in, w_out, rwkv_mu_r, rwkv_mu_k, rwkv_mu_v, rwkv_mu_w, rwkv_mu_a, rwkv_mu_g, rwkv_w0, rwkv_w2, rwkv_a0, rwkv_a2, rwkv_g2, rwkv_k_k, rwkv_k_a, rwkv_r_k, rwkv_gn_gain, rwkv_gn_bias, ffn2_norm, ffn2_w_gate, ffn2_w_up, ffn2_w_down, final_norm):
    b, s, d = x.shape
    assert b == 1 and d == D_MODEL and ffn1_norm.shape[0] == 1
    h = x[0]
    row = lambda a: a.reshape(1, -1).astype(F32)
    z64 = jnp.zeros((1, LORA), F32)

    tm_prep = 256
    tix = np.arange(tm_prep)
    same_chunk = (tix[:, None] // RW_CHUNK) == (tix[None, :] // RW_CHUNK)
    hix = np.arange(RWKV_WIDTH) // RWKV_HEAD_DIM
    prep = dict(
        mu_r=row(rwkv_mu_r[0]), mu_k=row(rwkv_mu_k[0]), mu_v=row(rwkv_mu_v[0]),
        mu_m=jnp.concatenate([z64, row(rwkv_mu_w[0]), row(rwkv_mu_a[0]), row(rwkv_mu_g[0])], axis=1),
        w0=row(rwkv_w0[0]), w2=_pad_rows(rwkv_w2[0], LORA),
        a0=row(rwkv_a0[0]), a2=_pad_rows(rwkv_a2[0], 0), g2=_pad_rows(rwkv_g2[0], LORA),
        k_k=row(rwkv_k_k[0]), k_a=row(rwkv_k_a[0]), r_k=row(rwkv_r_k[0]),
        bd=jnp.asarray(hix[:, None] == hix[None, :], BF16),
        tri=jnp.asarray(same_chunk & (tix[:, None] >= tix[None, :]), BF16),
        blk=jnp.asarray(same_chunk, BF16),
    )

    h1 = _ffn(h, row(ffn1_norm[0]), ffn1_w_gate[0].astype(BF16), ffn1_w_up[0].astype(BF16),
              ffn1_w_down[0].astype(BF16), row(final_norm), final_norm=False)
    u = _inproj(h1, row(mix_norm[0]), _relayout_w_in(w_in[0]))

    q, k, vt, qi, ki, wt = _attn_prep(u, _rope_tables(s, ATTN_HEAD_DIM), _rope_tables(s, IDX_HEAD_DIM),
                                     tm=DSA_KEY_TILE)
    attn = _dsa(q, qi, wt, k, vt, ki)

    mc, nc, yc, y0, bonus, g = _rwkv_prep(u, prep, tm=tm_prep)
    y = _rwkv_state(mc, nc, yc, y0)

    wo = w_out[0].astype(BF16)
    h2 = _outproj(h1, attn, y, bonus, g, row(rwkv_gn_gain[0]), row(rwkv_gn_bias[0]), prep["bd"],
                  wo[:ATTN_WIDTH], wo[ATTN_WIDTH:])
    out = _ffn(h2, row(ffn2_norm[0]), ffn2_w_gate[0].astype(BF16), ffn2_w_up[0].astype(BF16),
               ffn2_w_down[0].astype(BF16), row(final_norm), final_norm=True)
    return out[None]
```
